```python
import jax, jax.numpy as jnp
from jax import lax
import numpy as np

D_MODEL = 1024
BATCH = 32
SEQ = 2048
DEPTH = 1

HEAD_DIM = 64
N_ATT_HEADS = 6
N_KV_GROUPS = 2
ATT_WIDTH = N_ATT_HEADS * HEAD_DIM
KV_WIDTH = N_KV_GROUPS * HEAD_DIM
N_IDX_HEADS = 8
IDX_DIM = 32
TOPK_MAX = 256
Q_BLOCK = 128
CONV_CHANNELS = 384
SHORT_CONV_K = 3
N_CROSS_HEADS = 4
CROSS_WIDTH = N_CROSS_HEADS * HEAD_DIM
MEM_LEN = 256
N_BRANCHES = 3
D_FF = 2816
FFN_CONV_K = 3
ROPE_THETA = 500000.0
ROPE_FRACTION = 4
NORM_EPS = 1e-6

PROJ_SIZES = (ATT_WIDTH, KV_WIDTH, KV_WIDTH, N_IDX_HEADS * IDX_DIM, IDX_DIM, N_IDX_HEADS,
              3 * CONV_CHANNELS, CROSS_WIDTH, N_BRANCHES * D_MODEL)
PROJ_WIDTH = sum(PROJ_SIZES)

kernel_name = "hybrid_dsa_shortconv_memxattn_convffn"


def rms_norm(x, g):
    xf = x.astype(jnp.float32)
    y = xf * lax.rsqrt(jnp.mean(xf * xf, axis=-1, keepdims=True) + NORM_EPS)
    return (y * g.astype(jnp.float32)).astype(x.dtype)


def split_cols(a, sizes):
    out, off = [], 0
    for n in sizes:
        out.append(a[..., off:off + n])
        off += n
    return out


def rope_tables(positions, rot_dim, dtype):
    half = rot_dim // 2
    inv_freq = ROPE_THETA ** (-jnp.arange(half, dtype=jnp.float32) / half)
    ang = positions.astype(jnp.float32)[:, None] * inv_freq[None, :]
    return jnp.cos(ang).astype(dtype), jnp.sin(ang).astype(dtype)


def partial_rope(x, cos, sin):
    half = cos.shape[-1]
    shape = (1, cos.shape[0]) + (1,) * (x.ndim - 3) + (half,)
    c, s = cos.reshape(shape), sin.reshape(shape)
    x1, x2, rest = x[..., :half], x[..., half:2 * half], x[..., 2 * half:]
    return jnp.concatenate([x1 * c - x2 * s, x2 * c + x1 * s, rest], axis=-1)


def causal_dwconv(u, w):
    k, c = w.shape
    return lax.conv_general_dilated(u, w[:, None, :].astype(u.dtype), window_strides=(1,),
                                    padding=[(k - 1, 0)],
                                    dimension_numbers=("NWC", "WIO", "NWC"),
                                    feature_group_count=c)


def dsa_attention(q, k, v, qi, ki, wi):
    b, s = q.shape[:2]
    topk = min(TOPK_MAX, s // 4)
    blk = min(Q_BLOCK, s)
    nb = s // blk
    rep = N_ATT_HEADS // N_KV_GROUPS
    idx_scale = (N_IDX_HEADS * IDX_DIM) ** -0.5
    att_scale = HEAD_DIM ** -0.5
    key_pos = jnp.arange(s)
    gather = jax.vmap(lambda a, i: a[i])

    def to_blocks(a):
        return jnp.moveaxis(a.reshape((b, nb, blk) + a.shape[2:]), 1, 0)

    def one_block(args):
        q_b, qi_b, wi_b, start = args
        qpos = start + jnp.arange(blk)
        causal = key_pos[None, :] <= qpos[:, None]
        rel = jax.nn.relu(jnp.einsum('bqhd,bsd->bqhs', qi_b, ki).astype(jnp.float32))
        score = jnp.einsum('bqh,bqhs->bqs', wi_b.astype(jnp.float32), rel) * idx_scale
        score = jnp.where(causal[None], score, -jnp.inf)
        _, sel = lax.top_k(score, topk)
        k_sel = gather(k, sel)
        v_sel = gather(v, sel)
        qg = q_b.reshape(b, blk, N_KV_GROUPS, rep, HEAD_DIM)
        logits = jnp.einsum('bqgrd,bqkgd->bqgrk', qg, k_sel).astype(jnp.float32) * att_scale
        valid = sel <= qpos[None, :, None]
        logits = jnp.where(valid[:, :, None, None, :], logits, -jnp.inf)
        p = jax.nn.softmax(logits, axis=-1).astype(v.dtype)
        o = jnp.einsum('bqgrk,bqkgd->bqgrd', p, v_sel)
        return o.reshape(b, blk, N_ATT_HEADS * HEAD_DIM)

    starts = jnp.arange(nb) * blk
    out = lax.map(one_block, (to_blocks(q), to_blocks(qi), to_blocks(wi), starts))
    return jnp.moveaxis(out, 0, 1).reshape(b, s, N_ATT_HEADS * HEAD_DIM)


def memory_attention(qc, km, vm):
    b, s = qc.shape[:2]
    logits = jnp.einsum('bshd,bmhd->bhsm', qc, km).astype(jnp.float32) * HEAD_DIM ** -0.5
    p = jax.nn.softmax(logits, axis=-1).astype(vm.dtype)
    return jnp.einsum('bhsm,bmhd->bshd', p, vm).reshape(b, s, CROSS_WIDTH)


def setup_inputs(seed: int = 0) -> dict:
    key = jax.random.key(seed)
    ks = jax.random.split(key, 20)

    def w(k, shape, fan_in):
        return jax.random.normal(k, shape, jnp.float32) * fan_in ** -0.5

    def gain(k, shape):
        return 1.0 + 0.02 * jax.random.normal(k, shape, jnp.float32)

    return {
        "x": jax.random.normal(ks[0], (BATCH, SEQ, D_MODEL), jnp.float32),
        "mem": jax.random.normal(ks[1], (BATCH, MEM_LEN, D_MODEL), jnp.float32),
        "g_mix": gain(ks[2], (DEPTH, D_MODEL)),
        "w_in": w(ks[3], (DEPTH, D_MODEL, PROJ_WIDTH), D_MODEL),
        "b_gate": 0.02 * jax.random.normal(ks[4], (DEPTH, N_BRANCHES * D_MODEL), jnp.float32),
        "conv_w_short": w(ks[5], (DEPTH, SHORT_CONV_K, CONV_CHANNELS), SHORT_CONV_K),
        "w_att_out": w(ks[6], (DEPTH, ATT_WIDTH, D_MODEL), ATT_WIDTH),
        "w_conv_out": w(ks[7], (DEPTH, CONV_CHANNELS, D_MODEL), CONV_CHANNELS),
        "w_mem_out": w(ks[8], (DEPTH, CROSS_WIDTH, D_MODEL), CROSS_WIDTH),
        "w_o": w(ks[9], (DEPTH, D_MODEL, D_MODEL), D_MODEL),
        "g_mem": gain(ks[10], (DEPTH, D_MODEL)),
        "w_mem_kv": w(ks[11], (DEPTH, D_MODEL, 2 * CROSS_WIDTH), D_MODEL),
        "g_ffn": gain(ks[12], (DEPTH, D_MODEL)),
        "w_up": w(ks[13], (DEPTH, D_MODEL, 2 * D_FF), D_MODEL),
        "conv_w_ffn": w(ks[14], (DEPTH, FFN_CONV_K, 2 * D_FF), FFN_CONV_K),
        "w_down": w(ks[15], (DEPTH, D_FF, D_MODEL), D_FF),
        "g_final": gain(ks[16], (D_MODEL,)),
    }


def reference(x, mem, g_mix, w_in, b_gate, conv_w_short, w_att_out, w_conv_out, w_mem_out,
              w_o, g_mem, w_mem_kv, g_ffn, w_up, conv_w_ffn, w_down, g_final):
    b, s, d = x.shape
    m = mem.shape[1]
    positions = jnp.arange(s)
    cos_a, sin_a = rope_tables(positions, HEAD_DIM // ROPE_FRACTION, x.dtype)
    cos_i, sin_i = rope_tables(positions, IDX_DIM // ROPE_FRACTION, x.dtype)

    for l in range(DEPTH):
        h = rms_norm(x, g_mix[l])
        proj = h @ w_in[l]
        q, k, v, qi, ki, wi, conv_in, qc, gate_pre = split_cols(proj, PROJ_SIZES)

        q = partial_rope(q.reshape(b, s, N_ATT_HEADS, HEAD_DIM), cos_a, sin_a)
        k = partial_rope(k.reshape(b, s, N_KV_GROUPS, HEAD_DIM), cos_a, sin_a)
        v = v.reshape(b, s, N_KV_GROUPS, HEAD_DIM)
        qi = partial_rope(qi.reshape(b, s, N_IDX_HEADS, IDX_DIM), cos_i, sin_i)
        ki = partial_rope(ki, cos_i, sin_i)
        y_att = dsa_attention(q, k, v, qi, ki, wi) @ w_att_out[l]

        bg, cg, u = jnp.split(conv_in, 3, axis=-1)
        y_conv = (bg * causal_dwconv(cg * u, conv_w_short[l])) @ w_conv_out[l]

        mem_kv = (rms_norm(mem, g_mem[l]) @ w_mem_kv[l]).reshape(b, m, 2, N_CROSS_HEADS, HEAD_DIM)
        y_mem = memory_attention(qc.reshape(b, s, N_CROSS_HEADS, HEAD_DIM),
                                 mem_kv[:, :, 0], mem_kv[:, :, 1]) @ w_mem_out[l]

        gates = jax.nn.sigmoid(gate_pre + b_gate[l]).reshape(b, s, N_BRANCHES, d)
        merged = gates[:, :, 0] * y_att + gates[:, :, 1] * y_conv + gates[:, :, 2] * y_mem
        x = x + merged @ w_o[l]

        h = rms_norm(x, g_ffn[l])
        up = causal_dwconv(h @ w_up[l], conv_w_ffn[l])
        gate, val = jnp.split(up, 2, axis=-1)
        x = x + (jax.nn.silu(gate) * val) @ w_down[l]

    return rms_norm(x, g_final)
```

```python
import functools

import numpy as np
import jax
import jax.numpy as jnp
from jax import lax
from jax.experimental import pallas as pl
from jax.experimental.pallas import tpu as pltpu

F32 = jnp.float32
BF16 = jnp.bfloat16
I32 = jnp.int32

HEAD_DIM = 64
N_ATT_HEADS = 6
N_KV_GROUPS = 2
HEADS_PER_GROUP = N_ATT_HEADS // N_KV_GROUPS
ATT_WIDTH = N_ATT_HEADS * HEAD_DIM
KV_WIDTH = N_KV_GROUPS * HEAD_DIM
N_IDX_HEADS = 8
IDX_DIM = 32
IDX_WIDTH = N_IDX_HEADS * IDX_DIM
TOPK_MAX = 256
CONV_CHANNELS = 384
CONV_K = 3
N_CROSS_HEADS = 4
CROSS_WIDTH = N_CROSS_HEADS * HEAD_DIM
N_BRANCHES = 3
ROPE_THETA = 500000.0
ROPE_FRACTION = 4
NORM_EPS = 1e-6

LANES = 128
SUBLANES = 8
VMEM_LIMIT = 56 * 1024 * 1024

Q_TILE = 128
NEG_BIG = -1e30
INT_MIN = -2 ** 31
KEY_MIN_FINITE = -2139095040


def _rms(x, g):
    return x * lax.rsqrt(jnp.mean(x * x, axis=-1, keepdims=True) + NORM_EPS) * g


def _dot_nt(a, b):
    return lax.dot_general(a, b, (((1,), (1,)), ((), ())), preferred_element_type=F32)


def _dot(a, b):
    return jnp.dot(a, b, preferred_element_type=F32)


def _rope_tile(x, c, sa, sb, half):
    return x * c + pltpu.roll(x, LANES - half, 1) * sa + pltpu.roll(x, half, 1) * sb


def _rope(x, c, sa, sb, half):
    tiles = [_rope_tile(x[:, i:i + LANES], c, sa, sb, half) for i in range(0, x.shape[1], LANES)]
    return tiles[0] if len(tiles) == 1 else jnp.concatenate(tiles, axis=1)


_OFF_Q = 0
_OFF_WI = _OFF_Q + ATT_WIDTH
_OFF_KV = _OFF_WI + LANES
_OFF_QI = _OFF_KV + 2 * KV_WIDTH
_OFF_KI = _OFF_QI + IDX_WIDTH
_OFF_CONV = _OFF_KI + IDX_WIDTH
_OFF_QC = _OFF_CONV + 3 * CONV_CHANNELS
_W1_WIDTH = _OFF_QC + CROSS_WIDTH


def _in_proj_kernel(x_ref, g_ref, w_ref, ca_ref, saa_ref, sba_ref, ci_ref, sai_ref, sbi_ref,
                    q_ref, ks_ref, vt_ref, qi_ref, ki_ref, conv_ref, qc_ref, wi_ref, *, kchunk):
    h = _rms(x_ref[0], g_ref[...]).astype(BF16)
    ca, saa, sba = ca_ref[...], saa_ref[...], sba_ref[...]
    ci, sai, sbi = ci_ref[...], sai_ref[...], sbi_ref[...]
    half_a = HEAD_DIM // ROPE_FRACTION // 2
    half_i = IDX_DIM // ROPE_FRACTION // 2

    a = _dot(h, w_ref[:, _OFF_Q:_OFF_KV])
    q_ref[0] = _rope(a[:, :ATT_WIDTH], ca, saa, sba, half_a).astype(BF16)
    wi_ref[0] = a[:, ATT_WIDTH:]

    kv = _dot(h, w_ref[:, _OFF_KV:_OFF_QI])
    ks_ref[0] = _rope(kv[:, :KV_WIDTH], ca, saa, sba, half_a).astype(BF16)
    v = kv[:, KV_WIDTH:]
    for i in range(v.shape[0] // kchunk):
        vt_ref[0, i] = v[i * kchunk:(i + 1) * kchunk].T.astype(BF16)

    idx = _dot(h, w_ref[:, _OFF_QI:_OFF_CONV])
    qi_ref[0] = _rope(idx[:, :IDX_WIDTH], ci, sai, sbi, half_i).astype(BF16)
    ki_ref[0] = _rope(idx[:, IDX_WIDTH:], ci, sai, sbi, half_i).astype(BF16)

    cq = _dot(h, w_ref[:, _OFF_CONV:_W1_WIDTH])
    conv_ref[0] = cq[:, :3 * CONV_CHANNELS].astype(BF16)
    qc_ref[0] = cq[:, 3 * CONV_CHANNELS:].astype(BF16)


def _rope_tables(s, period, rot_dim):
    half = rot_dim // 2
    inv_freq = ROPE_THETA ** (-jnp.arange(half, dtype=F32) / half)
    ang = jnp.arange(s).astype(F32)[:, None] * inv_freq[None, :]
    cos, sin = jnp.cos(ang), jnp.sin(ang)
    j = np.arange(LANES) % period
    first = jnp.asarray(j < half)[None, :]
    second = jnp.asarray((j >= half) & (j < rot_dim))[None, :]
    fidx = np.where(j < half, j, np.where(j < rot_dim, j - half, 0))
    cos_l, sin_l = cos[:, fidx], sin[:, fidx]
    c = jnp.where(first | second, cos_l, 1.0)
    sa = jnp.where(first, -sin_l, 0.0)
    sb = jnp.where(second, sin_l, 0.0)
    return c, sa, sb


def _in_proj(x, g_mix, w1, tm, kchunk):
    b, s, d = x.shape
    tabs = _rope_tables(s, HEAD_DIM, HEAD_DIM // ROPE_FRACTION) + \
        _rope_tables(s, IDX_DIM, IDX_DIM // ROPE_FRACTION)
    tok = lambda w: pl.BlockSpec((1, tm, w), lambda si, bi: (bi, si, 0))
    tab = pl.BlockSpec((tm, LANES), lambda si, bi: (si, 0))
    const = lambda shape: pl.BlockSpec(shape, lambda si, bi: (0,) * len(shape),
                                       pipeline_mode=pl.Buffered(1))
    out_shape = (
        jax.ShapeDtypeStruct((b, s, ATT_WIDTH), BF16),
        jax.ShapeDtypeStruct((b, s, KV_WIDTH), BF16),
        jax.ShapeDtypeStruct((b, s // kchunk, KV_WIDTH, kchunk), BF16),
        jax.ShapeDtypeStruct((b, s, IDX_WIDTH), BF16),
        jax.ShapeDtypeStruct((b, s, IDX_WIDTH), BF16),
        jax.ShapeDtypeStruct((b, s, 3 * CONV_CHANNELS), BF16),
        jax.ShapeDtypeStruct((b, s, CROSS_WIDTH), BF16),
        jax.ShapeDtypeStruct((b, s, LANES), F32),
    )
    out_specs = (tok(ATT_WIDTH), tok(KV_WIDTH),
                 pl.BlockSpec((1, tm // kchunk, KV_WIDTH, kchunk), lambda si, bi: (bi, si, 0, 0)),
                 tok(IDX_WIDTH), tok(IDX_WIDTH), tok(3 * CONV_CHANNELS), tok(CROSS_WIDTH), tok(LANES))
    return pl.pallas_call(
        functools.partial(_in_proj_kernel, kchunk=kchunk),
        grid=(s // tm, b),
        in_specs=[tok(d), const((1, d)), const((d, _W1_WIDTH))] + [tab] * 6,
        out_specs=out_specs,
        out_shape=out_shape,
        compiler_params=pltpu.CompilerParams(
            dimension_semantics=("arbitrary", "arbitrary"), vmem_limit_bytes=VMEM_LIMIT),
        name="in_proj",
    )(x, g_mix, w1, *tabs)


def _mem_kv_kernel(mem_ref, g_ref, w_ref, km_ref, vm_ref):
    h = _rms(mem_ref[0], g_ref[...]).astype(BF16)
    kv = _dot(h, w_ref[...])
    km_ref[0] = kv[:, :CROSS_WIDTH].astype(BF16)
    vm_ref[0] = kv[:, CROSS_WIDTH:].astype(BF16)


def _mem_kv(mem, g_mem, w_kv):
    b, m, d = mem.shape
    const = lambda shape: pl.BlockSpec(shape, lambda bi: (0,) * len(shape))
    spec = pl.BlockSpec((1, m, CROSS_WIDTH), lambda bi: (bi, 0, 0))
    return pl.pallas_call(
        _mem_kv_kernel,
        grid=(b,),
        in_specs=[pl.BlockSpec((1, m, d), lambda bi: (bi, 0, 0)), const((1, d)),
                  const((d, 2 * CROSS_WIDTH))],
        out_specs=(spec, spec),
        out_shape=(jax.ShapeDtypeStruct((b, m, CROSS_WIDTH), BF16),) * 2,
        compiler_params=pltpu.CompilerParams(dimension_semantics=("arbitrary",)),
        name="mem_kv",
    )(mem, g_mem, w_kv)


def _key_to_float(key):
    return lax.bitcast_convert_type(key ^ ((key >> 31) & 0x7FFFFFFF), F32)


def _dsa_kernel(qi_ref, wi_ref, q_ref, ki_ref, ks_ref, vt_ref, o_ref, sc_ref, acc_ref,
                *, topk, ch, jbits):
    tq = Q_TILE
    q0 = pl.program_id(1) * tq
    nc = (q0 + tq + ch - 1) // ch
    qpos = q0 + lax.broadcasted_iota(I32, (1, tq), 1)

    qi = qi_ref[0]
    head_of_lane = lax.broadcasted_iota(I32, (tq, IDX_WIDTH), 1) // IDX_DIM
    zero_bf = jnp.zeros((), BF16)
    rhs = [jnp.concatenate([jnp.where(head_of_lane == 2 * j, qi, zero_bf),
                            jnp.where(head_of_lane == 2 * j + 1, qi, zero_bf)], axis=0)
           for j in range(N_IDX_HEADS // 2)]
    wit = wi_ref[0].T
    wrow = [wit[h:h + 1, :] for h in range(N_IDX_HEADS)]

    def chunk_pos(c):
        off = pl.multiple_of(c * ch, ch)
        return off, off + lax.broadcasted_iota(I32, (ch, tq), 0)

    def score_chunk(c, carry):
        off, kpos = chunk_pos(c)
        kc = ki_ref[0, pl.ds(off, ch), :]
        acc = jnp.zeros((ch, tq), F32)
        for j in range(N_IDX_HEADS // 2):
            s = _dot_nt(kc, rhs[j])
            acc = acc + jnp.maximum(s[:, :tq], 0.0) * wrow[2 * j]
            acc = acc + jnp.maximum(s[:, tq:], 0.0) * wrow[2 * j + 1]
        sc_ref[pl.ds(off, ch), :] = jnp.where(kpos <= qpos, acc, -jnp.inf)
        return carry

    lax.fori_loop(0, nc, score_chunk, 0)

    def count(pred):
        def body(c, cnt):
            off, kpos = chunk_pos(c)
            hit = jnp.where(pred(sc_ref[pl.ds(off, ch), :], kpos), 1, 0).astype(I32)
            return cnt + hit.reshape(ch // SUBLANES, SUBLANES, tq).sum(axis=0)
        cnt = lax.fori_loop(0, nc, body, jnp.zeros((SUBLANES, tq), I32))
        return cnt.sum(axis=0, keepdims=True)

    def key_bit(i, key):
        trial = key + lax.shift_left(jnp.int32(1), 31 - i)
        cand = _key_to_float(trial)
        return jnp.where(count(lambda s, _: s >= cand) >= topk, trial, key)

    key = lax.fori_loop(0, 32, key_bit, jnp.full((1, tq), INT_MIN, I32))
    tau = _key_to_float(jnp.maximum(key, KEY_MIN_FINITE))

    n_ge = count(lambda s, _: s >= tau)
    any_excess = jnp.max(jnp.where(n_ge > topk, 1, 0)) > 0

    @pl.when(any_excess)
    def _():
        want = topk - count(lambda s, _: s > tau)

        def pos_bit(i, jmax):
            trial = jmax + lax.shift_left(jnp.int32(1), jbits - 1 - i)
            n_before = count(lambda s, kpos: jnp.where(s == tau, kpos, trial) < trial)
            return jnp.where(n_before <= want, trial, jmax)

        jmax = lax.fori_loop(0, jbits, pos_bit, jnp.zeros((1, tq), I32))

        def drop(c, carry):
            off, kpos = chunk_pos(c)
            s = sc_ref[pl.ds(off, ch), :]
            late_tie = jnp.where(s == tau, kpos, -1) >= jmax
            sc_ref[pl.ds(off, ch), :] = jnp.where(late_tie, -jnp.inf, s)
            return carry

        lax.fori_loop(0, nc, drop, 0)

    q = q_ref[0]
    group_of_lane = lax.broadcasted_iota(I32, (tq, LANES), 1) // HEAD_DIM
    qe = []
    for h in range(N_ATT_HEADS):
        g = h // HEADS_PER_GROUP
        t = q[:, (h // 2) * LANES:(h // 2 + 1) * LANES].astype(F32)
        if h % 2 != g:
            t = pltpu.roll(t, HEAD_DIM, 1)
        qe.append(jnp.where(group_of_lane == g, t, 0.0).astype(BF16))
    qe = jnp.concatenate(qe, axis=0)
    nh = N_ATT_HEADS * tq

    acc_ref[...] = jnp.zeros((KV_WIDTH, nh), F32)

    def attend(c, carry):
        m, l = carry
        off, _ = chunk_pos(c)
        lg = _dot_nt(ks_ref[0, pl.ds(off, ch), :], qe)
        sel = sc_ref[pl.ds(off, ch), :] >= tau
        lg = jnp.concatenate([jnp.where(sel, lg[:, h * tq:(h + 1) * tq], NEG_BIG)
                              for h in range(N_ATT_HEADS)], axis=1)
        m_new = jnp.maximum(m, lg.max(axis=0, keepdims=True))
        alpha = jnp.exp(m - m_new)
        p = jnp.exp(lg - m_new)
        l = alpha * l + p.sum(axis=0, keepdims=True)
        acc_ref[...] = alpha * acc_ref[...] + _dot(vt_ref[0, c], p.astype(BF16))
        return m_new, l

    _, l = lax.fori_loop(0, nc, attend,
                         (jnp.full((1, nh), NEG_BIG, F32), jnp.zeros((1, nh), F32)))

    o = acc_ref[...] * (1.0 / l)
    lane = lax.broadcasted_iota(I32, (tq, LANES), 1)
    tiles = []
    for j in range(N_ATT_HEADS // 2):
        parts = []
        for p_ in range(2):
            h = 2 * j + p_
            blk = o[:, h * tq:(h + 1) * tq]
            if h // HEADS_PER_GROUP != p_:
                blk = jnp.concatenate([blk[HEAD_DIM:], blk[:HEAD_DIM]], axis=0)
            parts.append(blk.T)
        tiles.append(jnp.where(lane < HEAD_DIM, parts[0], parts[1]))
    o_ref[0] = jnp.concatenate(tiles, axis=1).astype(BF16)


def _dsa(qi, wi, q, ki, ks, vt, ch):
    b, s, _ = q.shape
    topk = min(TOPK_MAX, s // 4)
    blk = lambda w: pl.BlockSpec((1, Q_TILE, w), lambda bi, qb: (bi, qb, 0))
    full = lambda w: pl.BlockSpec((1, s, w), lambda bi, qb: (bi, 0, 0))
    return pl.pallas_call(
        functools.partial(_dsa_kernel, topk=topk, ch=ch, jbits=int(s).bit_length()),
        grid=(b, s // Q_TILE),
        in_specs=[blk(IDX_WIDTH), blk(LANES), blk(ATT_WIDTH), full(IDX_WIDTH), full(KV_WIDTH),
                  pl.BlockSpec((1, s // ch, KV_WIDTH, ch), lambda bi, qb: (bi, 0, 0, 0))],
        out_specs=blk(ATT_WIDTH),
        out_shape=jax.ShapeDtypeStruct((b, s, ATT_WIDTH), BF16),
        scratch_shapes=[pltpu.VMEM((s, Q_TILE), F32),
                        pltpu.VMEM((KV_WIDTH, N_ATT_HEADS * Q_TILE), F32)],
        compiler_params=pltpu.CompilerParams(
            dimension_semantics=("arbitrary", "arbitrary"), vmem_limit_bytes=VMEM_LIMIT),
        name="dsa",
    )(qi, wi, q, ki, ks, vt)


def _causal_conv3(u, w, carry_ref, first):
    @pl.when(first)
    def _():
        carry_ref[...] = jnp.zeros(carry_ref.shape, F32)

    prev = carry_ref[...]
    row = lax.broadcasted_iota(I32, prev.shape, 0)
    s1 = pltpu.roll(u, 1, 0)
    s2 = pltpu.roll(u, 2, 0)
    p1 = pltpu.roll(prev, 1, 0)
    p2 = pltpu.roll(prev, 2, 0)
    s1 = jnp.concatenate([jnp.where(row < 1, p1, s1[:SUBLANES]), s1[SUBLANES:]], axis=0)
    s2 = jnp.concatenate([jnp.where(row < 2, p2, s2[:SUBLANES]), s2[SUBLANES:]], axis=0)
    carry_ref[...] = u[u.shape[0] - SUBLANES:]
    return s2 * w[0:1] + s1 * w[1:2] + u * w[2:3]


def _merge_kernel(x_ref, g_ref, att_ref, conv_ref, qc_ref, km_ref, vm_ref, wg_ref, bg_ref, cw_ref,
                  wa_ref, wc_ref, wm_ref, wo_ref, o_ref, carry_ref):
    x = x_ref[0]
    h = _rms(x, g_ref[...]).astype(BF16)

    y_att = _dot(att_ref[0], wa_ref[...])

    cin = conv_ref[0].astype(F32)
    c = CONV_CHANNELS
    cu = cin[:, c:2 * c] * cin[:, 2 * c:]
    yc = cin[:, :c] * _causal_conv3(cu, cw_ref[...], carry_ref, pl.program_id(1) == 0)
    y_conv = _dot(yc.astype(BF16), wc_ref[...])

    qc, km, vm = qc_ref[0], km_ref[0], vm_ref[0]
    head_of_lane = lax.broadcasted_iota(I32, km.shape, 1) // HEAD_DIM
    zero_bf = jnp.zeros((), BF16)
    mem_out = jnp.zeros((x.shape[0], CROSS_WIDTH), F32)
    for hh in range(N_CROSS_HEADS):
        lg = _dot_nt(qc, jnp.where(head_of_lane == hh, km, zero_bf)) * (HEAD_DIM ** -0.5)
        e = jnp.exp(lg - lg.max(axis=-1, keepdims=True))
        p = e / e.sum(axis=-1, keepdims=True)
        mem_out = mem_out + _dot(p.astype(BF16), jnp.where(head_of_lane == hh, vm, zero_bf))
    y_mem = _dot(mem_out.astype(BF16), wm_ref[...])

    d = x.shape[1]
    merged = jnp.zeros(x.shape, F32)
    for i, y in enumerate((y_att, y_conv, y_mem)):
        pre = _dot(h, wg_ref[:, i * d:(i + 1) * d]) + bg_ref[:, i * d:(i + 1) * d]
        merged = merged + jax.nn.sigmoid(pre) * y
    o_ref[0] = x + _dot(merged.astype(BF16), wo_ref[...])


def _merge(x, g_mix, att, conv_in, qc, km, vm, wg, b_gate, conv_w, wa, wc, wm, wo, tm):
    b, s, d = x.shape
    m = km.shape[1]
    tok = lambda w: pl.BlockSpec((1, tm, w), lambda bi, si: (bi, si, 0))
    const = lambda shape: pl.BlockSpec(shape, lambda bi, si: (0,) * len(shape),
                                       pipeline_mode=pl.Buffered(1))
    memspec = pl.BlockSpec((1, m, CROSS_WIDTH), lambda bi, si: (bi, 0, 0))
    return pl.pallas_call(
        _merge_kernel,
        grid=(b, s // tm),
        in_specs=[tok(d), const((1, d)), tok(ATT_WIDTH), tok(3 * CONV_CHANNELS), tok(CROSS_WIDTH),
                  memspec, memspec, const((d, N_BRANCHES * d)), const((1, N_BRANCHES * d)),
                  const((CONV_K, CONV_CHANNELS)), const((ATT_WIDTH, d)), const((CONV_CHANNELS, d)),
                  const((CROSS_WIDTH, d)), const((d, d))],
        out_specs=tok(d),
        out_shape=jax.ShapeDtypeStruct((b, s, d), F32),
        scratch_shapes=[pltpu.VMEM((SUBLANES, CONV_CHANNELS), F32)],
        compiler_params=pltpu.CompilerParams(
            dimension_semantics=("arbitrary", "arbitrary"), vmem_limit_bytes=VMEM_LIMIT),
        name="merge",
    )(x, g_mix, att, conv_in, qc, km, vm, wg, b_gate, conv_w, wa, wc, wm, wo)


def _ffn_kernel(x_ref, g_ref, wu_ref, cw_ref, wd_ref, gf_ref, o_ref, carry_ref, act_ref, *, fc):
    x = x_ref[0]
    h = _rms(x, g_ref[...]).astype(BF16)
    dff = wd_ref.shape[0]
    first = pl.program_id(1) == 0
    for c0 in range(0, dff, fc):
        halves = []
        for base in (0, dff):
            cols = slice(base + c0, base + c0 + fc)
            up = _dot(h, wu_ref[:, cols])
            halves.append(_causal_conv3(up, cw_ref[:, cols], carry_ref.at[:, cols], first))
        gate, val = halves
        act_ref[:, c0:c0 + fc] = (gate * jax.nn.sigmoid(gate) * val).astype(BF16)
    y = x + _dot(act_ref[...], wd_ref[...])
    o_ref[0] = _rms(y, gf_ref[...])


def _ffn(x, g_ffn, wu, conv_w, wd, g_final, tm, fc):
    b, s, d = x.shape
    dff = wd.shape[0]
    tok = pl.BlockSpec((1, tm, d), lambda bi, si: (bi, si, 0))
    const = lambda shape: pl.BlockSpec(shape, lambda bi, si: (0,) * len(shape),
                                       pipeline_mode=pl.Buffered(1))
    return pl.pallas_call(
        functools.partial(_ffn_kernel, fc=fc),
        grid=(b, s // tm),
        in_specs=[tok, const((1, d)), const((d, 2 * dff)), const((CONV_K, 2 * dff)),
                  const((dff, d)), const((1, d))],
        out_specs=tok,
        out_shape=jax.ShapeDtypeStruct((b, s, d), F32),
        scratch_shapes=[pltpu.VMEM((SUBLANES, 2 * dff), F32), pltpu.VMEM((tm, dff), BF16)],
        compiler_params=pltpu.CompilerParams(
            dimension_semantics=("arbitrary", "arbitrary"), vmem_limit_bytes=VMEM_LIMIT),
        name="ffn",
    )(x, g_ffn, wu, conv_w, wd, g_final)


def _pack_in_proj_weight(w):
    sizes = (ATT_WIDTH, KV_WIDTH, KV_WIDTH, IDX_WIDTH, IDX_DIM, N_IDX_HEADS,
             3 * CONV_CHANNELS, CROSS_WIDTH)
    offs = np.concatenate([[0], np.cumsum(sizes)])
    wq, wk, wv, wqi, wki, wwi, wconv, wqc = [w[:, offs[i]:offs[i + 1]] for i in range(len(sizes))]
    idx_scale = (N_IDX_HEADS * IDX_DIM) ** -0.5
    att_scale = HEAD_DIM ** -0.5
    wwi = jnp.pad(wwi * idx_scale, ((0, 0), (0, LANES - N_IDX_HEADS)))
    w1 = jnp.concatenate([wq * att_scale, wwi, wk, wv, wqi, jnp.tile(wki, (1, N_IDX_HEADS)),
                          wconv, wqc], axis=1)
    return w1.astype(BF16), w[:, offs[-1]:].astype(BF16)


def _layer(x, mem, g_mix, w_in, b_gate, conv_w_short, w_att_out, w_conv_out, w_mem_out, w_o,
           g_mem, w_mem_kv, g_ffn, w_up, conv_w_ffn, w_down, g_final):
    b, s, d = x.shape
    tm1 = min(512, s)
    tm4 = min(256, s)
    tm5 = min(256, s)
    ch = min(256, s)
    w1, wg = _pack_in_proj_weight(w_in)
    q, ks, vt, qi, ki, conv_in, qc, wi = _in_proj(x, g_mix[None], w1, tm1, ch)
    km, vm = _mem_kv(mem, g_mem[None], w_mem_kv.astype(BF16))
    att = _dsa(qi, wi, q, ki, ks, vt, ch)
    x1 = _merge(x, g_mix[None], att, conv_in, qc, km, vm, wg, b_gate[None], conv_w_short,
                w_att_out.astype(BF16), w_conv_out.astype(BF16), w_mem_out.astype(BF16),
                w_o.astype(BF16), tm4)
    dff = w_down.shape[0]
    fc = 256 if dff % 256 == 0 else dff
    return _ffn(x1, g_ffn[None], w_up.astype(BF16), conv_w_ffn, w_down.astype(BF16),
                g_final[None], tm5, fc)


def kernel(x, mem, g_mix, w_in, b_gate, conv_w_short, w_att_out, w_conv_out, w_mem_out, w_o,
           g_mem, w_mem_kv, g_ffn, w_up, conv_w_ffn, w_down, g_final):
    depth = w_in.shape[0]
    assert depth == 1, "the final RMSNorm is fused into the last layer's ffn kernel"
    return _layer(x, mem, g_mix[0], w_in[0], b_gate[0], conv_w_short[0], w_att_out[0],
                  w_conv_out[0], w_mem_out[0], w_o[0], g_mem[0], w_mem_kv[0], g_ffn[0],
                  w_up[0], conv_w_ffn[0], w_down[0], g_final)
```

```python
import functools

import numpy as np
import jax
import jax.numpy as jnp
from jax import lax
from jax.experimental import pallas as pl
from jax.experimental.pallas import tpu as pltpu

F32 = jnp.float32
BF16 = jnp.bfloat16
I32 = jnp.int32

HEAD_DIM = 64
N_ATT_HEADS = 6
N_KV_GROUPS = 2
HEADS_PER_GROUP = N_ATT_HEADS // N_KV_GROUPS
ATT_WIDTH = N_ATT_HEADS * HEAD_DIM
KV_WIDTH = N_KV_GROUPS * HEAD_DIM
N_IDX_HEADS = 8
IDX_DIM = 32
IDX_WIDTH = N_IDX_HEADS * IDX_DIM
TOPK_MAX = 256
CONV_CHANNELS = 384
CONV_K = 3
N_CROSS_HEADS = 4
CROSS_WIDTH = N_CROSS_HEADS * HEAD_DIM
N_BRANCHES = 3
ROPE_THETA = 500000.0
ROPE_FRACTION = 4
NORM_EPS = 1e-6

LANES = 128
SUBLANES = 8
VMEM_LIMIT = 56 * 1024 * 1024

Q_TILE = 256
NEG_BIG = -1e30
INT_MIN = -2 ** 31
KEY_MIN_FINITE = -2139095040


def _rms(x, g):
    return x * lax.rsqrt(jnp.mean(x * x, axis=-1, keepdims=True) + NORM_EPS) * g


def _dot_nt(a, b):
    return lax.dot_general(a, b, (((1,), (1,)), ((), ())), preferred_element_type=F32)


def _dot(a, b):
    return jnp.dot(a, b, preferred_element_type=F32)


def _rope_tile(x, c, sa, sb, half):
    return x * c + pltpu.roll(x, LANES - half, 1) * sa + pltpu.roll(x, half, 1) * sb


def _rope(x, c, sa, sb, half):
    tiles = [_rope_tile(x[:, i:i + LANES], c, sa, sb, half) for i in range(0, x.shape[1], LANES)]
    return tiles[0] if len(tiles) == 1 else jnp.concatenate(tiles, axis=1)


_OFF_Q = 0
_OFF_WI = _OFF_Q + ATT_WIDTH
_OFF_KV = _OFF_WI + LANES
_OFF_QI = _OFF_KV + 2 * KV_WIDTH
_OFF_KI = _OFF_QI + IDX_WIDTH
_OFF_CONV = _OFF_KI + IDX_WIDTH
_OFF_QC = _OFF_CONV + 3 * CONV_CHANNELS
_W1_WIDTH = _OFF_QC + CROSS_WIDTH


def _in_proj_kernel(x_ref, g_ref, w_ref, ca_ref, saa_ref, sba_ref, ci_ref, sai_ref, sbi_ref,
                    q_ref, ks_ref, vt_ref, qi_ref, ki_ref, conv_ref, qc_ref, wi_ref, *, kchunk):
    h = _rms(x_ref[0], g_ref[...]).astype(BF16)
    ca, saa, sba = ca_ref[...], saa_ref[...], sba_ref[...]
    ci, sai, sbi = ci_ref[...], sai_ref[...], sbi_ref[...]
    half_a = HEAD_DIM // ROPE_FRACTION // 2
    half_i = IDX_DIM // ROPE_FRACTION // 2

    a = _dot(h, w_ref[:, _OFF_Q:_OFF_KV])
    q_ref[0] = _rope(a[:, :ATT_WIDTH], ca, saa, sba, half_a).astype(BF16)
    wi_ref[0] = a[:, ATT_WIDTH:]

    kv = _dot(h, w_ref[:, _OFF_KV:_OFF_QI])
    ks_ref[0] = _rope(kv[:, :KV_WIDTH], ca, saa, sba, half_a).astype(BF16)
    v = kv[:, KV_WIDTH:]
    for i in range(v.shape[0] // kchunk):
        vt_ref[0, i] = v[i * kchunk:(i + 1) * kchunk].T.astype(BF16)

    idx = _dot(h, w_ref[:, _OFF_QI:_OFF_CONV])
    qi_ref[0] = _rope(idx[:, :IDX_WIDTH], ci, sai, sbi, half_i).astype(BF16)
    ki_ref[0] = _rope(idx[:, IDX_WIDTH:], ci, sai, sbi, half_i).astype(BF16)

    cq = _dot(h, w_ref[:, _OFF_CONV:_W1_WIDTH])
    conv_ref[0] = cq[:, :3 * CONV_CHANNELS].astype(BF16)
    qc_ref[0] = cq[:, 3 * CONV_CHANNELS:].astype(BF16)


def _rope_tables(s, period, rot_dim):
    half = rot_dim // 2
    inv_freq = ROPE_THETA ** (-jnp.arange(half, dtype=F32) / half)
    ang = jnp.arange(s).astype(F32)[:, None] * inv_freq[None, :]
    cos, sin = jnp.cos(ang), jnp.sin(ang)
    j = np.arange(LANES) % period
    first = jnp.asarray(j < half)[None, :]
    second = jnp.asarray((j >= half) & (j < rot_dim))[None, :]
    fidx = np.where(j < half, j, np.where(j < rot_dim, j - half, 0))
    cos_l, sin_l = cos[:, fidx], sin[:, fidx]
    c = jnp.where(first | second, cos_l, 1.0)
    sa = jnp.where(first, -sin_l, 0.0)
    sb = jnp.where(second, sin_l, 0.0)
    return c, sa, sb


def _in_proj(x, g_mix, w1, tm, kchunk):
    b, s, d = x.shape
    tabs = _rope_tables(s, HEAD_DIM, HEAD_DIM // ROPE_FRACTION) + \
        _rope_tables(s, IDX_DIM, IDX_DIM // ROPE_FRACTION)
    tok = lambda w: pl.BlockSpec((1, tm, w), lambda si, bi: (bi, si, 0))
    tab = pl.BlockSpec((tm, LANES), lambda si, bi: (si, 0))
    const = lambda shape: pl.BlockSpec(shape, lambda si, bi: (0,) * len(shape),
                                       pipeline_mode=pl.Buffered(1))
    out_shape = (
        jax.ShapeDtypeStruct((b, s, ATT_WIDTH), BF16),
        jax.ShapeDtypeStruct((b, s, KV_WIDTH), BF16),
        jax.ShapeDtypeStruct((b, s // kchunk, KV_WIDTH, kchunk), BF16),
        jax.ShapeDtypeStruct((b, s, IDX_WIDTH), BF16),
        jax.ShapeDtypeStruct((b, s, IDX_WIDTH), BF16),
        jax.ShapeDtypeStruct((b, s, 3 * CONV_CHANNELS), BF16),
        jax.ShapeDtypeStruct((b, s, CROSS_WIDTH), BF16),
        jax.ShapeDtypeStruct((b, s, LANES), F32),
    )
    out_specs = (tok(ATT_WIDTH), tok(KV_WIDTH),
                 pl.BlockSpec((1, tm // kchunk, KV_WIDTH, kchunk), lambda si, bi: (bi, si, 0, 0)),
                 tok(IDX_WIDTH), tok(IDX_WIDTH), tok(3 * CONV_CHANNELS), tok(CROSS_WIDTH), tok(LANES))
    return pl.pallas_call(
        functools.partial(_in_proj_kernel, kchunk=kchunk),
        grid=(s // tm, b),
        in_specs=[tok(d), const((1, d)), const((d, _W1_WIDTH))] + [tab] * 6,
        out_specs=out_specs,
        out_shape=out_shape,
        compiler_params=pltpu.CompilerParams(
            dimension_semantics=("arbitrary", "arbitrary"), vmem_limit_bytes=VMEM_LIMIT),
        name="in_proj",
    )(x, g_mix, w1, *tabs)


def _mem_kv_kernel(mem_ref, g_ref, w_ref, km_ref, vm_ref):
    h = _rms(mem_ref[0], g_ref[...]).astype(BF16)
    kv = _dot(h, w_ref[...])
    km_ref[0] = kv[:, :CROSS_WIDTH].astype(BF16)
    vm_ref[0] = kv[:, CROSS_WIDTH:].astype(BF16)


def _mem_kv(mem, g_mem, w_kv):
    b, m, d = mem.shape
    const = lambda shape: pl.BlockSpec(shape, lambda bi: (0,) * len(shape))
    spec = pl.BlockSpec((1, m, CROSS_WIDTH), lambda bi: (bi, 0, 0))
    return pl.pallas_call(
        _mem_kv_kernel,
        grid=(b,),
        in_specs=[pl.BlockSpec((1, m, d), lambda bi: (bi, 0, 0)), const((1, d)),
                  const((d, 2 * CROSS_WIDTH))],
        out_specs=(spec, spec),
        out_shape=(jax.ShapeDtypeStruct((b, m, CROSS_WIDTH), BF16),) * 2,
        compiler_params=pltpu.CompilerParams(dimension_semantics=("arbitrary",)),
        name="mem_kv",
    )(mem, g_mem, w_kv)


def _key_to_float(key):
    return lax.bitcast_convert_type(key ^ ((key >> 31) & 0x7FFFFFFF), F32)


def _dsa_kernel(qi_ref, wi_ref, q_ref, ki_ref, ks_ref, vt_ref, o_ref, sc_ref, acc_ref,
                *, topk, jbits):
    tq = ch = Q_TILE
    qb = pl.program_id(1)
    nc = qb + 1
    qpos = qb * tq + lax.broadcasted_iota(I32, (1, tq), 1)

    def chunk_pos(c):
        off = pl.multiple_of(c * ch, ch)
        return off, off + lax.broadcasted_iota(I32, (ch, tq), 0)

    qi = qi_ref[0]
    head_of_lane = lax.broadcasted_iota(I32, (tq, IDX_WIDTH), 1) // IDX_DIM
    zero_bf = jnp.zeros((), BF16)
    qi_heads = [jnp.where(head_of_lane == h, qi, zero_bf) for h in range(N_IDX_HEADS)]
    wit = wi_ref[0].T
    wrow = [wit[h:h + 1, :] for h in range(N_IDX_HEADS)]

    def scores(c):
        off, kpos = chunk_pos(c)
        kc = ki_ref[0, pl.ds(off, ch), :]
        acc = jnp.maximum(_dot_nt(kc, qi_heads[0]), 0.0) * wrow[0]
        for h in range(1, N_IDX_HEADS):
            acc = acc + jnp.maximum(_dot_nt(kc, qi_heads[h]), 0.0) * wrow[h]
        return off, kpos, acc

    def score_chunk(c, carry):
        off, _, acc = scores(c)
        sc_ref[pl.ds(off, ch), :] = acc
        return carry

    lax.fori_loop(0, qb, score_chunk, 0)
    off, kpos, acc = scores(qb)
    sc_ref[pl.ds(off, ch), :] = jnp.where(kpos <= qpos, acc, -jnp.inf)

    def count(pred):
        def body(c, cnt):
            off, kpos = chunk_pos(c)
            hit = jnp.where(pred(sc_ref[pl.ds(off, ch), :], kpos), 1, 0).astype(I32)
            return cnt + hit.reshape(ch // SUBLANES, SUBLANES, tq).sum(axis=0)
        cnt = lax.fori_loop(0, nc, body, jnp.zeros((SUBLANES, tq), I32))
        return cnt.sum(axis=0, keepdims=True)

    def key_bit(i, key):
        trial = key + lax.shift_left(jnp.int32(1), 31 - i)
        cand = _key_to_float(trial)
        return jnp.where(count(lambda s, _: s >= cand) >= topk, trial, key)

    key = lax.fori_loop(0, 32, key_bit, jnp.full((1, tq), INT_MIN, I32))
    tau = _key_to_float(jnp.maximum(key, KEY_MIN_FINITE))

    n_ge = count(lambda s, _: s >= tau)
    any_excess = jnp.max(jnp.where(n_ge > topk, 1, 0)) > 0

    @pl.when(any_excess)
    def _():
        want = topk - count(lambda s, _: s > tau)

        def pos_bit(i, jmax):
            trial = jmax + lax.shift_left(jnp.int32(1), jbits - 1 - i)
            n_before = count(lambda s, kpos: jnp.where(s == tau, kpos, trial) < trial)
            return jnp.where(n_before <= want, trial, jmax)

        jmax = lax.fori_loop(0, jbits, pos_bit, jnp.zeros((1, tq), I32))

        def drop(c, carry):
            off, kpos = chunk_pos(c)
            s = sc_ref[pl.ds(off, ch), :]
            late_tie = jnp.where(s == tau, kpos, -1) >= jmax
            sc_ref[pl.ds(off, ch), :] = jnp.where(late_tie, -jnp.inf, s)
            return carry

        lax.fori_loop(0, nc, drop, 0)

    q = q_ref[0]
    group_of_lane = lax.broadcasted_iota(I32, (tq, LANES), 1) // HEAD_DIM
    qe = []
    for h in range(N_ATT_HEADS):
        g = h // HEADS_PER_GROUP
        t = q[:, (h // 2) * LANES:(h // 2 + 1) * LANES].astype(F32)
        if h % 2 != g:
            t = pltpu.roll(t, HEAD_DIM, 1)
        qe.append(jnp.where(group_of_lane == g, t, 0.0).astype(BF16))

    acc_ref[...] = jnp.zeros(acc_ref.shape, F32)

    def attend(c, carry):
        ms, ls = carry
        off, _ = chunk_pos(c)
        kc = ks_ref[0, pl.ds(off, ch), :]
        vt = vt_ref[0, c]
        bias = jnp.where(sc_ref[pl.ds(off, ch), :] >= tau, 0.0, NEG_BIG)
        new_ms, new_ls = [], []
        lgs = [_dot_nt(kc, qe[h]) for h in range(N_ATT_HEADS)]
        for h in range(N_ATT_HEADS):
            lg = bias + lgs[h]
            m_new = jnp.maximum(ms[h], lg.max(axis=0, keepdims=True))
            alpha = jnp.exp(ms[h] - m_new)
            p = jnp.exp(lg - m_new)
            new_ms.append(m_new)
            new_ls.append(alpha * ls[h] + p.sum(axis=0, keepdims=True))
            cols = slice(h * tq, (h + 1) * tq)
            acc_ref[:, cols] = alpha * acc_ref[:, cols] + _dot(vt, p.astype(BF16))
        return tuple(new_ms), tuple(new_ls)

    init = (tuple(jnp.full((1, tq), NEG_BIG, F32) for _ in range(N_ATT_HEADS)),
            tuple(jnp.zeros((1, tq), F32) for _ in range(N_ATT_HEADS)))
    _, ls = lax.fori_loop(0, nc, attend, init)

    lane = lax.broadcasted_iota(I32, (tq, LANES), 1)
    tiles = []
    for j in range(N_ATT_HEADS // 2):
        parts = []
        for p_ in range(2):
            h = 2 * j + p_
            blk = acc_ref[:, h * tq:(h + 1) * tq] * (1.0 / ls[h])
            if h // HEADS_PER_GROUP != p_:
                blk = jnp.concatenate([blk[HEAD_DIM:], blk[:HEAD_DIM]], axis=0)
            parts.append(blk.T)
        tiles.append(jnp.where(lane < HEAD_DIM, parts[0], parts[1]))
    o_ref[0] = jnp.concatenate(tiles, axis=1).astype(BF16)


def _dsa(qi, wi, q, ki, ks, vt):
    b, s, _ = q.shape
    topk = min(TOPK_MAX, s // 4)
    blk = lambda w: pl.BlockSpec((1, Q_TILE, w), lambda bi, qb: (bi, qb, 0))
    full = lambda w: pl.BlockSpec((1, s, w), lambda bi, qb: (bi, 0, 0))
    return pl.pallas_call(
        functools.partial(_dsa_kernel, topk=topk, jbits=int(s).bit_length()),
        grid=(b, s // Q_TILE),
        in_specs=[blk(IDX_WIDTH), blk(LANES), blk(ATT_WIDTH), full(IDX_WIDTH), full(KV_WIDTH),
                  pl.BlockSpec((1, s // Q_TILE, KV_WIDTH, Q_TILE), lambda bi, qb: (bi, 0, 0, 0))],
        out_specs=blk(ATT_WIDTH),
        out_shape=jax.ShapeDtypeStruct((b, s, ATT_WIDTH), BF16),
        scratch_shapes=[pltpu.VMEM((s, Q_TILE), F32),
                        pltpu.VMEM((KV_WIDTH, N_ATT_HEADS * Q_TILE), F32)],
        compiler_params=pltpu.CompilerParams(
            dimension_semantics=("arbitrary", "arbitrary"), vmem_limit_bytes=VMEM_LIMIT),
        name="dsa",
    )(qi, wi, q, ki, ks, vt)


def _causal_conv3(u, w, carry_ref):
    prev = carry_ref[...]
    row = lax.broadcasted_iota(I32, prev.shape, 0)
    s1 = pltpu.roll(u, 1, 0)
    s2 = pltpu.roll(u, 2, 0)
    p1 = pltpu.roll(prev, 1, 0)
    p2 = pltpu.roll(prev, 2, 0)
    s1 = jnp.concatenate([jnp.where(row < 1, p1, s1[:SUBLANES]), s1[SUBLANES:]], axis=0)
    s2 = jnp.concatenate([jnp.where(row < 2, p2, s2[:SUBLANES]), s2[SUBLANES:]], axis=0)
    carry_ref[...] = u[u.shape[0] - SUBLANES:]
    return s2 * w[0:1] + s1 * w[1:2] + u * w[2:3]


def _merge_kernel(x_ref, g_ref, att_ref, conv_ref, qc_ref, km_ref, vm_ref, wg_ref, bg_ref, cw_ref,
                  wa_ref, wc_ref, wm_ref, wo_ref, o_ref, carry_ref):
    @pl.when(pl.program_id(1) == 0)
    def _():
        carry_ref[...] = jnp.zeros(carry_ref.shape, F32)

    x = x_ref[0]
    h = _rms(x, g_ref[...]).astype(BF16)

    y_att = _dot(att_ref[0], wa_ref[...])

    cin = conv_ref[0].astype(F32)
    c = CONV_CHANNELS
    cu = cin[:, c:2 * c] * cin[:, 2 * c:]
    yc = cin[:, :c] * _causal_conv3(cu, cw_ref[...], carry_ref)
    y_conv = _dot(yc.astype(BF16), wc_ref[...])

    qc, km, vm = qc_ref[0], km_ref[0], vm_ref[0]
    head_of_lane = lax.broadcasted_iota(I32, km.shape, 1) // HEAD_DIM
    zero_bf = jnp.zeros((), BF16)
    mem_out = jnp.zeros((x.shape[0], CROSS_WIDTH), F32)
    for hh in range(N_CROSS_HEADS):
        lg = _dot_nt(qc, jnp.where(head_of_lane == hh, km, zero_bf)) * (HEAD_DIM ** -0.5)
        e = jnp.exp(lg - lg.max(axis=-1, keepdims=True))
        p = e / e.sum(axis=-1, keepdims=True)
        mem_out = mem_out + _dot(p.astype(BF16), jnp.where(head_of_lane == hh, vm, zero_bf))
    y_mem = _dot(mem_out.astype(BF16), wm_ref[...])

    d = x.shape[1]
    merged = jnp.zeros(x.shape, F32)
    for i, y in enumerate((y_att, y_conv, y_mem)):
        pre = _dot(h, wg_ref[:, i * d:(i + 1) * d]) + bg_ref[:, i * d:(i + 1) * d]
        merged = merged + jax.nn.sigmoid(pre) * y
    o_ref[0] = x + _dot(merged.astype(BF16), wo_ref[...])


def _merge(x, g_mix, att, conv_in, qc, km, vm, wg, b_gate, conv_w, wa, wc, wm, wo, tm):
    b, s, d = x.shape
    m = km.shape[1]
    tok = lambda w: pl.BlockSpec((1, tm, w), lambda bi, si: (bi, si, 0))
    const = lambda shape: pl.BlockSpec(shape, lambda bi, si: (0,) * len(shape),
                                       pipeline_mode=pl.Buffered(1))
    memspec = pl.BlockSpec((1, m, CROSS_WIDTH), lambda bi, si: (bi, 0, 0))
    return pl.pallas_call(
        _merge_kernel,
        grid=(b, s // tm),
        in_specs=[tok(d), const((1, d)), tok(ATT_WIDTH), tok(3 * CONV_CHANNELS), tok(CROSS_WIDTH),
                  memspec, memspec, const((d, N_BRANCHES * d)), const((1, N_BRANCHES * d)),
                  const((CONV_K, CONV_CHANNELS)), const((ATT_WIDTH, d)), const((CONV_CHANNELS, d)),
                  const((CROSS_WIDTH, d)), const((d, d))],
        out_specs=tok(d),
        out_shape=jax.ShapeDtypeStruct((b, s, d), F32),
        scratch_shapes=[pltpu.VMEM((SUBLANES, CONV_CHANNELS), F32)],
        compiler_params=pltpu.CompilerParams(
            dimension_semantics=("arbitrary", "arbitrary"), vmem_limit_bytes=VMEM_LIMIT),
        name="merge",
    )(x, g_mix, att, conv_in, qc, km, vm, wg, b_gate, conv_w, wa, wc, wm, wo)


def _ffn_kernel(x_ref, g_ref, wu_ref, cw_ref, wd_ref, gf_ref, o_ref, carry_ref, act_ref, *, fc):
    @pl.when(pl.program_id(1) == 0)
    def _():
        carry_ref[...] = jnp.zeros(carry_ref.shape, F32)

    x = x_ref[0]
    h = _rms(x, g_ref[...]).astype(BF16)
    dff = wd_ref.shape[0]
    for c0 in range(0, dff, fc):
        halves = []
        for base in (0, dff):
            cols = slice(base + c0, base + c0 + fc)
            up = _dot(h, wu_ref[:, cols])
            halves.append(_causal_conv3(up, cw_ref[:, cols], carry_ref.at[:, cols]))
        gate, val = halves
        act_ref[:, c0:c0 + fc] = (gate * jax.nn.sigmoid(gate) * val).astype(BF16)
    y = x + _dot(act_ref[...], wd_ref[...])
    o_ref[0] = _rms(y, gf_ref[...])


def _ffn(x, g_ffn, wu, conv_w, wd, g_final, tm, fc):
    b, s, d = x.shape
    dff = wd.shape[0]
    tok = pl.BlockSpec((1, tm, d), lambda bi, si: (bi, si, 0))
    const = lambda shape: pl.BlockSpec(shape, lambda bi, si: (0,) * len(shape),
                                       pipeline_mode=pl.Buffered(1))
    return pl.pallas_call(
        functools.partial(_ffn_kernel, fc=fc),
        grid=(b, s // tm),
        in_specs=[tok, const((1, d)), const((d, 2 * dff)), const((CONV_K, 2 * dff)),
                  const((dff, d)), const((1, d))],
        out_specs=tok,
        out_shape=jax.ShapeDtypeStruct((b, s, d), F32),
        scratch_shapes=[pltpu.VMEM((SUBLANES, 2 * dff), F32), pltpu.VMEM((tm, dff), BF16)],
        compiler_params=pltpu.CompilerParams(
            dimension_semantics=("arbitrary", "arbitrary"), vmem_limit_bytes=VMEM_LIMIT),
        name="ffn",
    )(x, g_ffn, wu, conv_w, wd, g_final)


def _pack_in_proj_weight(w):
    sizes = (ATT_WIDTH, KV_WIDTH, KV_WIDTH, IDX_WIDTH, IDX_DIM, N_IDX_HEADS,
             3 * CONV_CHANNELS, CROSS_WIDTH)
    offs = np.concatenate([[0], np.cumsum(sizes)])
    wq, wk, wv, wqi, wki, wwi, wconv, wqc = [w[:, offs[i]:offs[i + 1]] for i in range(len(sizes))]
    idx_scale = (N_IDX_HEADS * IDX_DIM) ** -0.5
    att_scale = HEAD_DIM ** -0.5
    wwi = jnp.pad(wwi * idx_scale, ((0, 0), (0, LANES - N_IDX_HEADS)))
    w1 = jnp.concatenate([wq * att_scale, wwi, wk, wv, wqi, jnp.tile(wki, (1, N_IDX_HEADS)),
                          wconv, wqc], axis=1)
    return w1.astype(BF16), w[:, offs[-1]:].astype(BF16)


def _layer(x, mem, g_mix, w_in, b_gate, conv_w_short, w_att_out, w_conv_out, w_mem_out, w_o,
           g_mem, w_mem_kv, g_ffn, w_up, conv_w_ffn, w_down, g_final):
    b, s, d = x.shape
    tm1 = min(512, s)
    tm4 = min(256, s)
    tm5 = min(256, s)
    w1, wg = _pack_in_proj_weight(w_in)
    q, ks, vt, qi, ki, conv_in, qc, wi = _in_proj(x, g_mix[None], w1, tm1, Q_TILE)
    km, vm = _mem_kv(mem, g_mem[None], w_mem_kv.astype(BF16))
    att = _dsa(qi, wi, q, ki, ks, vt)
    x1 = _merge(x, g_mix[None], att, conv_in, qc, km, vm, wg, b_gate[None], conv_w_short,
                w_att_out.astype(BF16), w_conv_out.astype(BF16), w_mem_out.astype(BF16),
                w_o.astype(BF16), tm4)
    dff = w_down.shape[0]
    fc = 256 if dff % 256 == 0 else dff
    return _ffn(x1, g_ffn[None], w_up.astype(BF16), conv_w_ffn, w_down.astype(BF16),
                g_final[None], tm5, fc)


def kernel(x, mem, g_mix, w_in, b_gate, conv_w_short, w_att_out, w_conv_out, w_mem_out, w_o,
           g_mem, w_mem_kv, g_ffn, w_up, conv_w_ffn, w_down, g_final):
    depth = w_in.shape[0]
    assert depth == 1, "the final RMSNorm is fused into the last layer's ffn kernel"
    return _layer(x, mem, g_mix[0], w_in[0], b_gate[0], conv_w_short[0], w_att_out[0],
                  w_conv_out[0], w_mem_out[0], w_o[0], g_mem[0], w_mem_kv[0], g_ffn[0],
                  w_up[0], conv_w_ffn[0], w_down[0], g_final)
```

```python
import functools

import numpy as np
import jax
import jax.numpy as jnp
from jax import lax
from jax.experimental import pallas as pl
from jax.experimental.pallas import tpu as pltpu

F32 = jnp.float32
BF16 = jnp.bfloat16
I32 = jnp.int32

HEAD_DIM = 64
N_ATT_HEADS = 6
N_KV_GROUPS = 2
HEADS_PER_GROUP = N_ATT_HEADS // N_KV_GROUPS
ATT_WIDTH = N_ATT_HEADS * HEAD_DIM
KV_WIDTH = N_KV_GROUPS * HEAD_DIM
N_IDX_HEADS = 8
IDX_DIM = 32
IDX_WIDTH = N_IDX_HEADS * IDX_DIM
TOPK_MAX = 256
CONV_CHANNELS = 384
CONV_K = 3
N_CROSS_HEADS = 4
CROSS_WIDTH = N_CROSS_HEADS * HEAD_DIM
N_BRANCHES = 3
ROPE_THETA = 500000.0
ROPE_FRACTION = 4
NORM_EPS = 1e-6

LANES = 128
SUBLANES = 8
VMEM_LIMIT = 56 * 1024 * 1024

Q_TILE = 256
NEG_BIG = -1e30
INT_MIN = -2 ** 31
KEY_MIN_FINITE = -2139095040


def _rms(x, g):
    return x * lax.rsqrt(jnp.mean(x * x, axis=-1, keepdims=True) + NORM_EPS) * g


def _dot_nt(a, b):
    return lax.dot_general(a, b, (((1,), (1,)), ((), ())), preferred_element_type=F32)


def _dot(a, b):
    return jnp.dot(a, b, preferred_element_type=F32)


def _rope_tile(x, c, sa, sb, half):
    return x * c + pltpu.roll(x, LANES - half, 1) * sa + pltpu.roll(x, half, 1) * sb


def _rope(x, c, sa, sb, half):
    tiles = [_rope_tile(x[:, i:i + LANES], c, sa, sb, half) for i in range(0, x.shape[1], LANES)]
    return tiles[0] if len(tiles) == 1 else jnp.concatenate(tiles, axis=1)


_OFF_Q = 0
_OFF_WI = _OFF_Q + ATT_WIDTH
_OFF_KV = _OFF_WI + LANES
_OFF_QI = _OFF_KV + 2 * KV_WIDTH
_OFF_KI = _OFF_QI + IDX_WIDTH
_OFF_CONV = _OFF_KI + IDX_WIDTH
_OFF_QC = _OFF_CONV + 3 * CONV_CHANNELS
_W1_WIDTH = _OFF_QC + CROSS_WIDTH


def _in_proj_kernel(x_ref, g_ref, w_ref, ca_ref, saa_ref, sba_ref, ci_ref, sai_ref, sbi_ref,
                    q_ref, ks_ref, vt_ref, qi_ref, ki_ref, conv_ref, qc_ref, wi_ref, *, kchunk):
    h = _rms(x_ref[0], g_ref[...]).astype(BF16)
    ca, saa, sba = ca_ref[...], saa_ref[...], sba_ref[...]
    ci, sai, sbi = ci_ref[...], sai_ref[...], sbi_ref[...]
    half_a = HEAD_DIM // ROPE_FRACTION // 2
    half_i = IDX_DIM // ROPE_FRACTION // 2

    a = _dot(h, w_ref[:, _OFF_Q:_OFF_KV])
    q_ref[0] = _rope(a[:, :ATT_WIDTH], ca, saa, sba, half_a).astype(BF16)
    wi_ref[0] = a[:, ATT_WIDTH:]

    kv = _dot(h, w_ref[:, _OFF_KV:_OFF_QI])
    ks_ref[0] = _rope(kv[:, :KV_WIDTH], ca, saa, sba, half_a).astype(BF16)
    v = kv[:, KV_WIDTH:]
    for i in range(v.shape[0] // kchunk):
        vt_ref[0, i] = v[i * kchunk:(i + 1) * kchunk].T.astype(BF16)

    idx = _dot(h, w_ref[:, _OFF_QI:_OFF_CONV])
    qi_ref[0] = _rope(idx[:, :IDX_WIDTH], ci, sai, sbi, half_i).astype(BF16)
    ki_ref[0] = _rope(idx[:, IDX_WIDTH:], ci, sai, sbi, half_i).astype(BF16)

    cq = _dot(h, w_ref[:, _OFF_CONV:_W1_WIDTH])
    conv_ref[0] = cq[:, :3 * CONV_CHANNELS].astype(BF16)
    qc_ref[0] = cq[:, 3 * CONV_CHANNELS:].astype(BF16)


def _rope_tables(s, period, rot_dim):
    half = rot_dim // 2
    inv_freq = ROPE_THETA ** (-jnp.arange(half, dtype=F32) / half)
    ang = jnp.arange(s).astype(F32)[:, None] * inv_freq[None, :]
    cos, sin = jnp.cos(ang), jnp.sin(ang)
    j = np.arange(LANES) % period
    first = jnp.asarray(j < half)[None, :]
    second = jnp.asarray((j >= half) & (j < rot_dim))[None, :]
    fidx = np.where(j < half, j, np.where(j < rot_dim, j - half, 0))
    cos_l, sin_l = cos[:, fidx], sin[:, fidx]
    c = jnp.where(first | second, cos_l, 1.0)
    sa = jnp.where(first, -sin_l, 0.0)
    sb = jnp.where(second, sin_l, 0.0)
    return c, sa, sb


def _in_proj(x, g_mix, w1, tm, kchunk):
    b, s, d = x.shape
    tabs = _rope_tables(s, HEAD_DIM, HEAD_DIM // ROPE_FRACTION) + \
        _rope_tables(s, IDX_DIM, IDX_DIM // ROPE_FRACTION)
    tok = lambda w: pl.BlockSpec((1, tm, w), lambda si, bi: (bi, si, 0))
    tab = pl.BlockSpec((tm, LANES), lambda si, bi: (si, 0))
    const = lambda shape: pl.BlockSpec(shape, lambda si, bi: (0,) * len(shape),
                                       pipeline_mode=pl.Buffered(1))
    out_shape = (
        jax.ShapeDtypeStruct((b, s, ATT_WIDTH), BF16),
        jax.ShapeDtypeStruct((b, s, KV_WIDTH), BF16),
        jax.ShapeDtypeStruct((b, s // kchunk, KV_WIDTH, kchunk), BF16),
        jax.ShapeDtypeStruct((b, s, IDX_WIDTH), BF16),
        jax.ShapeDtypeStruct((b, s, IDX_WIDTH), BF16),
        jax.ShapeDtypeStruct((b, s, 3 * CONV_CHANNELS), BF16),
        jax.ShapeDtypeStruct((b, s, CROSS_WIDTH), BF16),
        jax.ShapeDtypeStruct((b, s, LANES), F32),
    )
    out_specs = (tok(ATT_WIDTH), tok(KV_WIDTH),
                 pl.BlockSpec((1, tm // kchunk, KV_WIDTH, kchunk), lambda si, bi: (bi, si, 0, 0)),
                 tok(IDX_WIDTH), tok(IDX_WIDTH), tok(3 * CONV_CHANNELS), tok(CROSS_WIDTH), tok(LANES))
    return pl.pallas_call(
        functools.partial(_in_proj_kernel, kchunk=kchunk),
        grid=(s // tm, b),
        in_specs=[tok(d), const((1, d)), const((d, _W1_WIDTH))] + [tab] * 6,
        out_specs=out_specs,
        out_shape=out_shape,
        compiler_params=pltpu.CompilerParams(
            dimension_semantics=("arbitrary", "arbitrary"), vmem_limit_bytes=VMEM_LIMIT),
        name="in_proj",
    )(x, g_mix, w1, *tabs)


def _mem_kv_kernel(mem_ref, g_ref, w_ref, km_ref, vm_ref):
    h = _rms(mem_ref[0], g_ref[...]).astype(BF16)
    kv = _dot(h, w_ref[...])
    km_ref[0] = kv[:, :CROSS_WIDTH].astype(BF16)
    vm_ref[0] = kv[:, CROSS_WIDTH:].astype(BF16)


def _mem_kv(mem, g_mem, w_kv):
    b, m, d = mem.shape
    const = lambda shape: pl.BlockSpec(shape, lambda bi: (0,) * len(shape))
    spec = pl.BlockSpec((1, m, CROSS_WIDTH), lambda bi: (bi, 0, 0))
    return pl.pallas_call(
        _mem_kv_kernel,
        grid=(b,),
        in_specs=[pl.BlockSpec((1, m, d), lambda bi: (bi, 0, 0)), const((1, d)),
                  const((d, 2 * CROSS_WIDTH))],
        out_specs=(spec, spec),
        out_shape=(jax.ShapeDtypeStruct((b, m, CROSS_WIDTH), BF16),) * 2,
        compiler_params=pltpu.CompilerParams(dimension_semantics=("arbitrary",)),
        name="mem_kv",
    )(mem, g_mem, w_kv)


def _key_to_float(key):
    return lax.bitcast_convert_type(key ^ ((key >> 31) & 0x7FFFFFFF), F32)


def _dsa_kernel(*refs, n_qblocks, **kw):
    for qb in range(n_qblocks):
        pl.when(pl.program_id(1) == qb)(functools.partial(_dsa_block, qb, *refs, **kw))


def _dsa_block(qb, qi_ref, wi_ref, q_ref, ki_ref, ks_ref, vt_ref, o_ref, sc_ref, acc_ref,
               *, topk, jbits):
    tq = ch = Q_TILE
    nc = qb + 1
    qpos = qb * tq + lax.broadcasted_iota(I32, (1, tq), 1)

    def chunk_pos(c):
        off = c * ch
        return off, off + lax.broadcasted_iota(I32, (ch, tq), 0)

    qi = qi_ref[0]
    head_of_lane = lax.broadcasted_iota(I32, (tq, IDX_WIDTH), 1) // IDX_DIM
    zero_bf = jnp.zeros((), BF16)
    qi_heads = [jnp.where(head_of_lane == h, qi, zero_bf) for h in range(N_IDX_HEADS)]
    wit = wi_ref[0].T
    wrow = [wit[h:h + 1, :] for h in range(N_IDX_HEADS)]

    def scores(c):
        off, kpos = chunk_pos(c)
        kc = ki_ref[0, pl.ds(off, ch), :]
        acc = jnp.maximum(_dot_nt(kc, qi_heads[0]), 0.0) * wrow[0]
        for h in range(1, N_IDX_HEADS):
            acc = acc + jnp.maximum(_dot_nt(kc, qi_heads[h]), 0.0) * wrow[h]
        return off, kpos, acc

    for c in range(nc):
        off, kpos, acc = scores(c)
        sc_ref[pl.ds(off, ch), :] = acc if c < qb else jnp.where(kpos <= qpos, acc, -jnp.inf)

    def count(pred):
        parts = []
        for c in range(nc):
            off, kpos = chunk_pos(c)
            hit = jnp.where(pred(sc_ref[pl.ds(off, ch), :], kpos), 1.0, 0.0)
            parts.append(hit.reshape(ch // SUBLANES, SUBLANES, tq).sum(axis=0))
        while len(parts) > 1:
            parts = [a + b for a, b in zip(parts[::2], parts[1::2])] + parts[len(parts) & ~1:]
        return parts[0].sum(axis=0, keepdims=True)

    def key_bit(i, key):
        trial = key + lax.shift_left(jnp.int32(1), 31 - i)
        cand = _key_to_float(trial)
        return jnp.where(count(lambda s, _: s >= cand) >= topk, trial, key)

    key = lax.fori_loop(0, 32, key_bit, jnp.full((1, tq), INT_MIN, I32))
    tau = _key_to_float(jnp.maximum(key, KEY_MIN_FINITE))

    n_ge = count(lambda s, _: s >= tau)
    any_excess = jnp.max(jnp.where(n_ge > topk, 1, 0)) > 0

    @pl.when(any_excess)
    def _():
        want = topk - count(lambda s, _: s > tau)

        def pos_bit(i, jmax):
            trial = jmax + lax.shift_left(jnp.int32(1), jbits - 1 - i)
            n_before = count(lambda s, kpos: jnp.where(s == tau, kpos, trial) < trial)
            return jnp.where(n_before <= want, trial, jmax)

        jmax = lax.fori_loop(0, jbits, pos_bit, jnp.zeros((1, tq), I32))

        for c in range(nc):
            off, kpos = chunk_pos(c)
            s = sc_ref[pl.ds(off, ch), :]
            late_tie = jnp.where(s == tau, kpos, -1) >= jmax
            sc_ref[pl.ds(off, ch), :] = jnp.where(late_tie, -jnp.inf, s)

    q = q_ref[0]
    group_of_lane = lax.broadcasted_iota(I32, (tq, LANES), 1) // HEAD_DIM
    qe = []
    for h in range(N_ATT_HEADS):
        g = h // HEADS_PER_GROUP
        t = q[:, (h // 2) * LANES:(h // 2 + 1) * LANES].astype(F32)
        if h % 2 != g:
            t = pltpu.roll(t, HEAD_DIM, 1)
        qe.append(jnp.where(group_of_lane == g, t, 0.0).astype(BF16))

    ms = [jnp.full((1, tq), NEG_BIG, F32) for _ in range(N_ATT_HEADS)]
    ls = [jnp.zeros((1, tq), F32) for _ in range(N_ATT_HEADS)]
    for c in range(nc):
        off, _ = chunk_pos(c)
        kc = ks_ref[0, pl.ds(off, ch), :]
        vt = vt_ref[0, c]
        bias = jnp.where(sc_ref[pl.ds(off, ch), :] >= tau, 0.0, NEG_BIG)
        lgs = [_dot_nt(kc, qe[h]) for h in range(N_ATT_HEADS)]
        for h in range(N_ATT_HEADS):
            lg = bias + lgs[h]
            m_new = jnp.maximum(ms[h], lg.max(axis=0, keepdims=True))
            alpha = jnp.exp(ms[h] - m_new)
            p = jnp.exp(lg - m_new)
            ms[h] = m_new
            ls[h] = alpha * ls[h] + p.sum(axis=0, keepdims=True)
            cols = slice(h * tq, (h + 1) * tq)
            pv = _dot(vt, p.astype(BF16))
            acc_ref[:, cols] = pv if c == 0 else alpha * acc_ref[:, cols] + pv

    lane = lax.broadcasted_iota(I32, (tq, LANES), 1)
    tiles = []
    for j in range(N_ATT_HEADS // 2):
        parts = []
        for p_ in range(2):
            h = 2 * j + p_
            blk = acc_ref[:, h * tq:(h + 1) * tq] * (1.0 / ls[h])
            if h // HEADS_PER_GROUP != p_:
                blk = jnp.concatenate([blk[HEAD_DIM:], blk[:HEAD_DIM]], axis=0)
            parts.append(blk.T)
        tiles.append(jnp.where(lane < HEAD_DIM, parts[0], parts[1]))
    o_ref[0] = jnp.concatenate(tiles, axis=1).astype(BF16)


def _dsa(qi, wi, q, ki, ks, vt):
    b, s, _ = q.shape
    topk = min(TOPK_MAX, s // 4)
    blk = lambda w: pl.BlockSpec((1, Q_TILE, w), lambda bi, qb: (bi, qb, 0))
    full = lambda w: pl.BlockSpec((1, s, w), lambda bi, qb: (bi, 0, 0))
    return pl.pallas_call(
        functools.partial(_dsa_kernel, n_qblocks=s // Q_TILE, topk=topk, jbits=int(s).bit_length()),
        grid=(b, s // Q_TILE),
        in_specs=[blk(IDX_WIDTH), blk(LANES), blk(ATT_WIDTH), full(IDX_WIDTH), full(KV_WIDTH),
                  pl.BlockSpec((1, s // Q_TILE, KV_WIDTH, Q_TILE), lambda bi, qb: (bi, 0, 0, 0))],
        out_specs=blk(ATT_WIDTH),
        out_shape=jax.ShapeDtypeStruct((b, s, ATT_WIDTH), BF16),
        scratch_shapes=[pltpu.VMEM((s, Q_TILE), F32),
                        pltpu.VMEM((KV_WIDTH, N_ATT_HEADS * Q_TILE), F32)],
        compiler_params=pltpu.CompilerParams(
            dimension_semantics=("arbitrary", "arbitrary"), vmem_limit_bytes=VMEM_LIMIT),
        name="dsa",
    )(qi, wi, q, ki, ks, vt)


def _causal_conv3(u, w, carry_ref):
    prev = carry_ref[...]
    row = lax.broadcasted_iota(I32, prev.shape, 0)
    s1 = pltpu.roll(u, 1, 0)
    s2 = pltpu.roll(u, 2, 0)
    p1 = pltpu.roll(prev, 1, 0)
    p2 = pltpu.roll(prev, 2, 0)
    s1 = jnp.concatenate([jnp.where(row < 1, p1, s1[:SUBLANES]), s1[SUBLANES:]], axis=0)
    s2 = jnp.concatenate([jnp.where(row < 2, p2, s2[:SUBLANES]), s2[SUBLANES:]], axis=0)
    carry_ref[...] = u[u.shape[0] - SUBLANES:]
    return s2 * w[0:1] + s1 * w[1:2] + u * w[2:3]


def _merge_kernel(x_ref, g_ref, att_ref, conv_ref, qc_ref, km_ref, vm_ref, wg_ref, bg_ref, cw_ref,
                  wa_ref, wc_ref, wm_ref, wo_ref, o_ref, carry_ref):
    @pl.when(pl.program_id(1) == 0)
    def _():
        carry_ref[...] = jnp.zeros(carry_ref.shape, F32)

    x = x_ref[0]
    h = _rms(x, g_ref[...]).astype(BF16)

    y_att = _dot(att_ref[0], wa_ref[...])

    cin = conv_ref[0].astype(F32)
    c = CONV_CHANNELS
    cu = cin[:, c:2 * c] * cin[:, 2 * c:]
    yc = cin[:, :c] * _causal_conv3(cu, cw_ref[...], carry_ref)
    y_conv = _dot(yc.astype(BF16), wc_ref[...])

    qc, km, vm = qc_ref[0], km_ref[0], vm_ref[0]
    head_of_lane = lax.broadcasted_iota(I32, km.shape, 1) // HEAD_DIM
    zero_bf = jnp.zeros((), BF16)
    mem_out = jnp.zeros((x.shape[0], CROSS_WIDTH), F32)
    for hh in range(N_CROSS_HEADS):
        lg = _dot_nt(qc, jnp.where(head_of_lane == hh, km, zero_bf)) * (HEAD_DIM ** -0.5)
        e = jnp.exp(lg - lg.max(axis=-1, keepdims=True))
        p = e / e.sum(axis=-1, keepdims=True)
        mem_out = mem_out + _dot(p.astype(BF16), jnp.where(head_of_lane == hh, vm, zero_bf))
    y_mem = _dot(mem_out.astype(BF16), wm_ref[...])

    d = x.shape[1]
    merged = jnp.zeros(x.shape, F32)
    for i, y in enumerate((y_att, y_conv, y_mem)):
        pre = _dot(h, wg_ref[:, i * d:(i + 1) * d]) + bg_ref[:, i * d:(i + 1) * d]
        merged = merged + jax.nn.sigmoid(pre) * y
    o_ref[0] = x + _dot(merged.astype(BF16), wo_ref[...])


def _merge(x, g_mix, att, conv_in, qc, km, vm, wg, b_gate, conv_w, wa, wc, wm, wo, tm):
    b, s, d = x.shape
    m = km.shape[1]
    tok = lambda w: pl.BlockSpec((1, tm, w), lambda bi, si: (bi, si, 0))
    const = lambda shape: pl.BlockSpec(shape, lambda bi, si: (0,) * len(shape),
                                       pipeline_mode=pl.Buffered(1))
    memspec = pl.BlockSpec((1, m, CROSS_WIDTH), lambda bi, si: (bi, 0, 0))
    return pl.pallas_call(
        _merge_kernel,
        grid=(b, s // tm),
        in_specs=[tok(d), const((1, d)), tok(ATT_WIDTH), tok(3 * CONV_CHANNELS), tok(CROSS_WIDTH),
                  memspec, memspec, const((d, N_BRANCHES * d)), const((1, N_BRANCHES * d)),
                  const((CONV_K, CONV_CHANNELS)), const((ATT_WIDTH, d)), const((CONV_CHANNELS, d)),
                  const((CROSS_WIDTH, d)), const((d, d))],
        out_specs=tok(d),
        out_shape=jax.ShapeDtypeStruct((b, s, d), F32),
        scratch_shapes=[pltpu.VMEM((SUBLANES, CONV_CHANNELS), F32)],
        compiler_params=pltpu.CompilerParams(
            dimension_semantics=("arbitrary", "arbitrary"), vmem_limit_bytes=VMEM_LIMIT),
        name="merge",
    )(x, g_mix, att, conv_in, qc, km, vm, wg, b_gate, conv_w, wa, wc, wm, wo)


def _ffn_kernel(x_ref, g_ref, wu_ref, cw_ref, wd_ref, gf_ref, o_ref, carry_ref, act_ref, *, fc):
    @pl.when(pl.program_id(1) == 0)
    def _():
        carry_ref[...] = jnp.zeros(carry_ref.shape, F32)

    x = x_ref[0]
    h = _rms(x, g_ref[...]).astype(BF16)
    dff = wd_ref.shape[0]
    for c0 in range(0, dff, fc):
        halves = []
        for base in (0, dff):
            cols = slice(base + c0, base + c0 + fc)
            up = _dot(h, wu_ref[:, cols])
            halves.append(_causal_conv3(up, cw_ref[:, cols], carry_ref.at[:, cols]))
        gate, val = halves
        act_ref[:, c0:c0 + fc] = (gate * jax.nn.sigmoid(gate) * val).astype(BF16)
    y = x + _dot(act_ref[...], wd_ref[...])
    o_ref[0] = _rms(y, gf_ref[...])


def _ffn(x, g_ffn, wu, conv_w, wd, g_final, tm, fc):
    b, s, d = x.shape
    dff = wd.shape[0]
    tok = pl.BlockSpec((1, tm, d), lambda bi, si: (bi, si, 0))
    const = lambda shape: pl.BlockSpec(shape, lambda bi, si: (0,) * len(shape),
                                       pipeline_mode=pl.Buffered(1))
    return pl.pallas_call(
        functools.partial(_ffn_kernel, fc=fc),
        grid=(b, s // tm),
        in_specs=[tok, const((1, d)), const((d, 2 * dff)), const((CONV_K, 2 * dff)),
                  const((dff, d)), const((1, d))],
        out_specs=tok,
        out_shape=jax.ShapeDtypeStruct((b, s, d), F32),
        scratch_shapes=[pltpu.VMEM((SUBLANES, 2 * dff), F32), pltpu.VMEM((tm, dff), BF16)],
        compiler_params=pltpu.CompilerParams(
            dimension_semantics=("arbitrary", "arbitrary"), vmem_limit_bytes=VMEM_LIMIT),
        name="ffn",
    )(x, g_ffn, wu, conv_w, wd, g_final)


def _pack_in_proj_weight(w):
    sizes = (ATT_WIDTH, KV_WIDTH, KV_WIDTH, IDX_WIDTH, IDX_DIM, N_IDX_HEADS,
             3 * CONV_CHANNELS, CROSS_WIDTH)
    offs = np.concatenate([[0], np.cumsum(sizes)])
    wq, wk, wv, wqi, wki, wwi, wconv, wqc = [w[:, offs[i]:offs[i + 1]] for i in range(len(sizes))]
    idx_scale = (N_IDX_HEADS * IDX_DIM) ** -0.5
    att_scale = HEAD_DIM ** -0.5
    wwi = jnp.pad(wwi * idx_scale, ((0, 0), (0, LANES - N_IDX_HEADS)))
    w1 = jnp.concatenate([wq * att_scale, wwi, wk, wv, wqi, jnp.tile(wki, (1, N_IDX_HEADS)),
                          wconv, wqc], axis=1)
    return w1.astype(BF16), w[:, offs[-1]:].astype(BF16)


def _layer(x, mem, g_mix, w_in, b_gate, conv_w_short, w_att_out, w_conv_out, w_mem_out, w_o,
           g_mem, w_mem_kv, g_ffn, w_up, conv_w_ffn, w_down, g_final):
    b, s, d = x.shape
    tm1 = min(512, s)
    tm4 = min(512, s)
    tm5 = min(512, s)
    w1, wg = _pack_in_proj_weight(w_in)
    q, ks, vt, qi, ki, conv_in, qc, wi = _in_proj(x, g_mix[None], w1, tm1, Q_TILE)
    km, vm = _mem_kv(mem, g_mem[None], w_mem_kv.astype(BF16))
    att = _dsa(qi, wi, q, ki, ks, vt)
    x1 = _merge(x, g_mix[None], att, conv_in, qc, km, vm, wg, b_gate[None], conv_w_short,
                w_att_out.astype(BF16), w_conv_out.astype(BF16), w_mem_out.astype(BF16),
                w_o.astype(BF16), tm4)
    dff = w_down.shape[0]
    fc = 256 if dff % 256 == 0 else dff
    return _ffn(x1, g_ffn[None], w_up.astype(BF16), conv_w_ffn, w_down.astype(BF16),
                g_final[None], tm5, fc)


def kernel(x, mem, g_mix, w_in, b_gate, conv_w_short, w_att_out, w_conv_out, w_mem_out, w_o,
           g_mem, w_mem_kv, g_ffn, w_up, conv_w_ffn, w_down, g_final):
    depth = w_in.shape[0]
    assert depth == 1, "the final RMSNorm is fused into the last layer's ffn kernel"
    return _layer(x, mem, g_mix[0], w_in[0], b_gate[0], conv_w_short[0], w_att_out[0],
                  w_conv_out[0], w_mem_out[0], w_o[0], g_mem[0], w_mem_kv[0], g_ffn[0],
                  w_up[0], conv_w_ffn[0], w_down[0], g_final)
```

```python
import functools

import numpy as np
import jax
import jax.numpy as jnp
from jax import lax
from jax.experimental import pallas as pl
from jax.experimental.pallas import tpu as pltpu

F32 = jnp.float32
BF16 = jnp.bfloat16
I32 = jnp.int32

HEAD_DIM = 64
N_ATT_HEADS = 6
N_KV_GROUPS = 2
HEADS_PER_GROUP = N_ATT_HEADS // N_KV_GROUPS
ATT_WIDTH = N_ATT_HEADS * HEAD_DIM
KV_WIDTH = N_KV_GROUPS * HEAD_DIM
N_IDX_HEADS = 8
IDX_DIM = 32
IDX_WIDTH = N_IDX_HEADS * IDX_DIM
TOPK_MAX = 256
CONV_CHANNELS = 384
CONV_K = 3
N_CROSS_HEADS = 4
CROSS_WIDTH = N_CROSS_HEADS * HEAD_DIM
N_BRANCHES = 3
ROPE_THETA = 500000.0
ROPE_FRACTION = 4
NORM_EPS = 1e-6

LANES = 128
SUBLANES = 8
VMEM_LIMIT = 56 * 1024 * 1024

Q_TILE = 256
NEG_BIG = -1e30
LOG2_E = 1.4426950408889634
INT_MIN = -2 ** 31
KEY_MIN_FINITE = -2139095040


def _rms(x, g):
    return x * lax.rsqrt(jnp.mean(x * x, axis=-1, keepdims=True) + NORM_EPS) * g


def _dot_nt(a, b):
    return lax.dot_general(a, b, (((1,), (1,)), ((), ())), preferred_element_type=F32)


def _dot(a, b):
    return jnp.dot(a, b, preferred_element_type=F32)


def _rope_tile(x, c, sa, sb, half):
    return x * c + pltpu.roll(x, LANES - half, 1) * sa + pltpu.roll(x, half, 1) * sb


def _rope(x, c, sa, sb, half):
    tiles = [_rope_tile(x[:, i:i + LANES], c, sa, sb, half) for i in range(0, x.shape[1], LANES)]
    return tiles[0] if len(tiles) == 1 else jnp.concatenate(tiles, axis=1)


_OFF_Q = 0
_OFF_WI = _OFF_Q + ATT_WIDTH
_OFF_KV = _OFF_WI + LANES
_OFF_QI = _OFF_KV + 2 * KV_WIDTH
_OFF_KI = _OFF_QI + IDX_WIDTH
_OFF_CONV = _OFF_KI + IDX_WIDTH
_OFF_QC = _OFF_CONV + 3 * CONV_CHANNELS
_W1_WIDTH = _OFF_QC + CROSS_WIDTH


def _in_proj_kernel(x_ref, g_ref, w_ref, ca_ref, saa_ref, sba_ref, ci_ref, sai_ref, sbi_ref,
                    q_ref, ks_ref, vt_ref, qi_ref, ki_ref, conv_ref, qc_ref, wi_ref, *, kchunk):
    h = _rms(x_ref[0], g_ref[...]).astype(BF16)
    ca, saa, sba = ca_ref[...], saa_ref[...], sba_ref[...]
    ci, sai, sbi = ci_ref[...], sai_ref[...], sbi_ref[...]
    half_a = HEAD_DIM // ROPE_FRACTION // 2
    half_i = IDX_DIM // ROPE_FRACTION // 2

    a = _dot(h, w_ref[:, _OFF_Q:_OFF_KV])
    q_ref[0] = (_rope(a[:, :ATT_WIDTH], ca, saa, sba, half_a) * LOG2_E).astype(BF16)
    wi_ref[0] = a[:, ATT_WIDTH:]

    kv = _dot(h, w_ref[:, _OFF_KV:_OFF_QI])
    ks_ref[0] = _rope(kv[:, :KV_WIDTH], ca, saa, sba, half_a).astype(BF16)
    v = kv[:, KV_WIDTH:]
    for i in range(v.shape[0] // kchunk):
        vt_ref[0, i] = v[i * kchunk:(i + 1) * kchunk].T.astype(BF16)

    idx = _dot(h, w_ref[:, _OFF_QI:_OFF_CONV])
    qi_ref[0] = _rope(idx[:, :IDX_WIDTH], ci, sai, sbi, half_i).astype(BF16)
    ki_ref[0] = _rope(idx[:, IDX_WIDTH:], ci, sai, sbi, half_i).astype(BF16)

    cq = _dot(h, w_ref[:, _OFF_CONV:_W1_WIDTH])
    conv_ref[0] = cq[:, :3 * CONV_CHANNELS].astype(BF16)
    qc_ref[0] = cq[:, 3 * CONV_CHANNELS:].astype(BF16)


def _rope_tables(s, period, rot_dim):
    half = rot_dim // 2
    inv_freq = ROPE_THETA ** (-jnp.arange(half, dtype=F32) / half)
    ang = jnp.arange(s).astype(F32)[:, None] * inv_freq[None, :]
    cos, sin = jnp.cos(ang), jnp.sin(ang)
    j = np.arange(LANES) % period
    first = jnp.asarray(j < half)[None, :]
    second = jnp.asarray((j >= half) & (j < rot_dim))[None, :]
    fidx = np.where(j < half, j, np.where(j < rot_dim, j - half, 0))
    cos_l, sin_l = cos[:, fidx], sin[:, fidx]
    c = jnp.where(first | second, cos_l, 1.0)
    sa = jnp.where(first, -sin_l, 0.0)
    sb = jnp.where(second, sin_l, 0.0)
    return c, sa, sb


def _in_proj(x, g_mix, w1, tm, kchunk):
    b, s, d = x.shape
    tabs = _rope_tables(s, HEAD_DIM, HEAD_DIM // ROPE_FRACTION) + \
        _rope_tables(s, IDX_DIM, IDX_DIM // ROPE_FRACTION)
    tok = lambda w: pl.BlockSpec((1, tm, w), lambda si, bi: (bi, si, 0))
    tab = pl.BlockSpec((tm, LANES), lambda si, bi: (si, 0))
    const = lambda shape: pl.BlockSpec(shape, lambda si, bi: (0,) * len(shape),
                                       pipeline_mode=pl.Buffered(1))
    out_shape = (
        jax.ShapeDtypeStruct((b, s, ATT_WIDTH), BF16),
        jax.ShapeDtypeStruct((b, s, KV_WIDTH), BF16),
        jax.ShapeDtypeStruct((b, s // kchunk, KV_WIDTH, kchunk), BF16),
        jax.ShapeDtypeStruct((b, s, IDX_WIDTH), BF16),
        jax.ShapeDtypeStruct((b, s, IDX_WIDTH), BF16),
        jax.ShapeDtypeStruct((b, s, 3 * CONV_CHANNELS), BF16),
        jax.ShapeDtypeStruct((b, s, CROSS_WIDTH), BF16),
        jax.ShapeDtypeStruct((b, s, LANES), F32),
    )
    out_specs = (tok(ATT_WIDTH), tok(KV_WIDTH),
                 pl.BlockSpec((1, tm // kchunk, KV_WIDTH, kchunk), lambda si, bi: (bi, si, 0, 0)),
                 tok(IDX_WIDTH), tok(IDX_WIDTH), tok(3 * CONV_CHANNELS), tok(CROSS_WIDTH), tok(LANES))
    return pl.pallas_call(
        functools.partial(_in_proj_kernel, kchunk=kchunk),
        grid=(s // tm, b),
        in_specs=[tok(d), const((1, d)), const((d, _W1_WIDTH))] + [tab] * 6,
        out_specs=out_specs,
        out_shape=out_shape,
        compiler_params=pltpu.CompilerParams(
            dimension_semantics=("arbitrary", "arbitrary"), vmem_limit_bytes=VMEM_LIMIT),
        name="in_proj",
    )(x, g_mix, w1, *tabs)


def _mem_kv_kernel(mem_ref, g_ref, w_ref, km_ref, vm_ref):
    h = _rms(mem_ref[0], g_ref[...]).astype(BF16)
    kv = _dot(h, w_ref[...])
    km_ref[0] = kv[:, :CROSS_WIDTH].astype(BF16)
    vm_ref[0] = kv[:, CROSS_WIDTH:].astype(BF16)


def _mem_kv(mem, g_mem, w_kv):
    b, m, d = mem.shape
    const = lambda shape: pl.BlockSpec(shape, lambda bi: (0,) * len(shape))
    spec = pl.BlockSpec((1, m, CROSS_WIDTH), lambda bi: (bi, 0, 0))
    return pl.pallas_call(
        _mem_kv_kernel,
        grid=(b,),
        in_specs=[pl.BlockSpec((1, m, d), lambda bi: (bi, 0, 0)), const((1, d)),
                  const((d, 2 * CROSS_WIDTH))],
        out_specs=(spec, spec),
        out_shape=(jax.ShapeDtypeStruct((b, m, CROSS_WIDTH), BF16),) * 2,
        compiler_params=pltpu.CompilerParams(dimension_semantics=("arbitrary",)),
        name="mem_kv",
    )(mem, g_mem, w_kv)


def _key_to_float(key):
    return lax.bitcast_convert_type(key ^ ((key >> 31) & 0x7FFFFFFF), F32)


def _for_chunks(n, body, init):
    def pair(i, carry):
        return body(2 * i + 1, body(2 * i, carry))
    carry = lax.fori_loop(0, n // 2, pair, init)
    return lax.cond(n % 2 == 1, lambda cr: body(n - 1, cr), lambda cr: cr, carry)


def _dsa_kernel(qi_ref, wi_ref, q_ref, ki_ref, ks_ref, vt_ref, o_ref, sc_ref, acc_ref,
                *, topk, jbits):
    tq = ch = Q_TILE
    qb = pl.program_id(1)
    nc = qb + 1
    qpos = qb * tq + lax.broadcasted_iota(I32, (1, tq), 1)

    def chunk_pos(c):
        off = pl.multiple_of(c * ch, ch)
        return off, off + lax.broadcasted_iota(I32, (ch, tq), 0)

    qi = qi_ref[0]
    head_of_lane = lax.broadcasted_iota(I32, (tq, IDX_WIDTH), 1) // IDX_DIM
    zero_bf = jnp.zeros((), BF16)
    qi_heads = [jnp.where(head_of_lane == h, qi, zero_bf) for h in range(N_IDX_HEADS)]
    wit = wi_ref[0].T
    wrow = [wit[h:h + 1, :] for h in range(N_IDX_HEADS)]

    def scores(c):
        off, kpos = chunk_pos(c)
        kc = ki_ref[0, pl.ds(off, ch), :]
        acc = jnp.maximum(_dot_nt(kc, qi_heads[0]), 0.0) * wrow[0]
        for h in range(1, N_IDX_HEADS):
            acc = acc + jnp.maximum(_dot_nt(kc, qi_heads[h]), 0.0) * wrow[h]
        return off, kpos, acc

    def score_chunk(c, carry):
        off, _, acc = scores(c)
        sc_ref[pl.ds(off, ch), :] = acc
        return carry

    _for_chunks(qb, score_chunk, 0)
    off, kpos, acc = scores(qb)
    sc_ref[pl.ds(off, ch), :] = jnp.where(kpos <= qpos, acc, -jnp.inf)

    def count(pred):
        def body(c, cnt):
            off, kpos = chunk_pos(c)
            hit = jnp.where(pred(sc_ref[pl.ds(off, ch), :], kpos), 1.0, 0.0)
            return cnt + hit.reshape(ch // SUBLANES, SUBLANES, tq).sum(axis=0)
        cnt = _for_chunks(nc, body, jnp.zeros((SUBLANES, tq), F32))
        return cnt.sum(axis=0, keepdims=True)

    def key_bit(i, key):
        trial = key + lax.shift_left(jnp.int32(1), 31 - i)
        cand = _key_to_float(trial)
        return jnp.where(count(lambda s, _: s >= cand) >= topk, trial, key)

    key = lax.fori_loop(0, 32, key_bit, jnp.full((1, tq), INT_MIN, I32))
    tau = _key_to_float(jnp.maximum(key, KEY_MIN_FINITE))

    n_ge = count(lambda s, _: s >= tau)
    any_excess = jnp.max(jnp.where(n_ge > topk, 1, 0)) > 0

    @pl.when(any_excess)
    def _():
        want = topk - count(lambda s, _: s > tau)

        def pos_bit(i, jmax):
            trial = jmax + lax.shift_left(jnp.int32(1), jbits - 1 - i)
            n_before = count(lambda s, kpos: jnp.where(s == tau, kpos, trial) < trial)
            return jnp.where(n_before <= want, trial, jmax)

        jmax = lax.fori_loop(0, jbits, pos_bit, jnp.zeros((1, tq), I32))

        def drop(c, carry):
            off, kpos = chunk_pos(c)
            s = sc_ref[pl.ds(off, ch), :]
            late_tie = jnp.where(s == tau, kpos, -1) >= jmax
            sc_ref[pl.ds(off, ch), :] = jnp.where(late_tie, -jnp.inf, s)
            return carry

        lax.fori_loop(0, nc, drop, 0)

    q = q_ref[0]
    group_of_lane = lax.broadcasted_iota(I32, (tq, LANES), 1) // HEAD_DIM
    qe = []
    for h in range(N_ATT_HEADS):
        g = h // HEADS_PER_GROUP
        t = q[:, (h // 2) * LANES:(h // 2 + 1) * LANES].astype(F32)
        if h % 2 != g:
            t = pltpu.roll(t, HEAD_DIM, 1)
        qe.append(jnp.where(group_of_lane == g, t, 0.0).astype(BF16))

    acc_ref[...] = jnp.zeros(acc_ref.shape, F32)

    def attend(c, carry):
        ms, ls = carry
        off, _ = chunk_pos(c)
        kc = ks_ref[0, pl.ds(off, ch), :]
        vt = vt_ref[0, c]
        bias = jnp.where(sc_ref[pl.ds(off, ch), :] >= tau, 0.0, NEG_BIG)
        new_ms, new_ls = [], []
        lgs = [_dot_nt(kc, qe[h]) for h in range(N_ATT_HEADS)]
        for h in range(N_ATT_HEADS):
            lg = bias + lgs[h]
            m_new = jnp.maximum(ms[h], lg.max(axis=0, keepdims=True))
            alpha = jnp.exp2(ms[h] - m_new)
            p = jnp.exp2(lg - m_new)
            new_ms.append(m_new)
            new_ls.append(alpha * ls[h] + p.sum(axis=0, keepdims=True))
            cols = slice(h * tq, (h + 1) * tq)
            acc_ref[:, cols] = alpha * acc_ref[:, cols] + _dot(vt, p.astype(BF16))
        return tuple(new_ms), tuple(new_ls)

    init = (tuple(jnp.full((1, tq), NEG_BIG, F32) for _ in range(N_ATT_HEADS)),
            tuple(jnp.zeros((1, tq), F32) for _ in range(N_ATT_HEADS)))
    _, ls = _for_chunks(nc, attend, init)

    lane = lax.broadcasted_iota(I32, (tq, LANES), 1)
    tiles = []
    for j in range(N_ATT_HEADS // 2):
        parts = []
        for p_ in range(2):
            h = 2 * j + p_
            blk = acc_ref[:, h * tq:(h + 1) * tq] * (1.0 / ls[h])
            if h // HEADS_PER_GROUP != p_:
                blk = jnp.concatenate([blk[HEAD_DIM:], blk[:HEAD_DIM]], axis=0)
            parts.append(blk.T)
        tiles.append(jnp.where(lane < HEAD_DIM, parts[0], parts[1]))
    o_ref[0] = jnp.concatenate(tiles, axis=1).astype(BF16)


def _dsa(qi, wi, q, ki, ks, vt):
    b, s, _ = q.shape
    topk = min(TOPK_MAX, s // 4)
    blk = lambda w: pl.BlockSpec((1, Q_TILE, w), lambda bi, qb: (bi, qb, 0))
    full = lambda w: pl.BlockSpec((1, s, w), lambda bi, qb: (bi, 0, 0))
    return pl.pallas_call(
        functools.partial(_dsa_kernel, topk=topk, jbits=int(s).bit_length()),
        grid=(b, s // Q_TILE),
        in_specs=[blk(IDX_WIDTH), blk(LANES), blk(ATT_WIDTH), full(IDX_WIDTH), full(KV_WIDTH),
                  pl.BlockSpec((1, s // Q_TILE, KV_WIDTH, Q_TILE), lambda bi, qb: (bi, 0, 0, 0))],
        out_specs=blk(ATT_WIDTH),
        out_shape=jax.ShapeDtypeStruct((b, s, ATT_WIDTH), BF16),
        scratch_shapes=[pltpu.VMEM((s, Q_TILE), F32),
                        pltpu.VMEM((KV_WIDTH, N_ATT_HEADS * Q_TILE), F32)],
        compiler_params=pltpu.CompilerParams(
            dimension_semantics=("arbitrary", "arbitrary"), vmem_limit_bytes=VMEM_LIMIT),
        name="dsa",
    )(qi, wi, q, ki, ks, vt)


def _causal_conv3(u, w, carry_ref):
    prev = carry_ref[...]
    row = lax.broadcasted_iota(I32, prev.shape, 0)
    s1 = pltpu.roll(u, 1, 0)
    s2 = pltpu.roll(u, 2, 0)
    p1 = pltpu.roll(prev, 1, 0)
    p2 = pltpu.roll(prev, 2, 0)
    s1 = jnp.concatenate([jnp.where(row < 1, p1, s1[:SUBLANES]), s1[SUBLANES:]], axis=0)
    s2 = jnp.concatenate([jnp.where(row < 2, p2, s2[:SUBLANES]), s2[SUBLANES:]], axis=0)
    carry_ref[...] = u[u.shape[0] - SUBLANES:]
    return s2 * w[0:1] + s1 * w[1:2] + u * w[2:3]


def _merge_kernel(x_ref, g_ref, att_ref, conv_ref, qc_ref, km_ref, vm_ref, wg_ref, bg_ref, cw_ref,
                  wa_ref, wc_ref, wm_ref, wo_ref, o_ref, carry_ref):
    @pl.when(pl.program_id(1) == 0)
    def _():
        carry_ref[...] = jnp.zeros(carry_ref.shape, F32)

    x = x_ref[0]
    h = _rms(x, g_ref[...]).astype(BF16)

    y_att = _dot(att_ref[0], wa_ref[...])

    cin = conv_ref[0].astype(F32)
    c = CONV_CHANNELS
    cu = cin[:, c:2 * c] * cin[:, 2 * c:]
    yc = cin[:, :c] * _causal_conv3(cu, cw_ref[...], carry_ref)
    y_conv = _dot(yc.astype(BF16), wc_ref[...])

    qc, km, vm = qc_ref[0], km_ref[0], vm_ref[0]
    head_of_lane = lax.broadcasted_iota(I32, km.shape, 1) // HEAD_DIM
    zero_bf = jnp.zeros((), BF16)
    mem_out = jnp.zeros((x.shape[0], CROSS_WIDTH), F32)
    for hh in range(N_CROSS_HEADS):
        lg = _dot_nt(qc, jnp.where(head_of_lane == hh, km, zero_bf)) * (HEAD_DIM ** -0.5)
        e = jnp.exp(lg - lg.max(axis=-1, keepdims=True))
        p = e / e.sum(axis=-1, keepdims=True)
        mem_out = mem_out + _dot(p.astype(BF16), jnp.where(head_of_lane == hh, vm, zero_bf))
    y_mem = _dot(mem_out.astype(BF16), wm_ref[...])

    d = x.shape[1]
    merged = jnp.zeros(x.shape, F32)
    for i, y in enumerate((y_att, y_conv, y_mem)):
        pre = _dot(h, wg_ref[:, i * d:(i + 1) * d]) + bg_ref[:, i * d:(i + 1) * d]
        merged = merged + jax.nn.sigmoid(pre) * y
    o_ref[0] = x + _dot(merged.astype(BF16), wo_ref[...])


def _merge(x, g_mix, att, conv_in, qc, km, vm, wg, b_gate, conv_w, wa, wc, wm, wo, tm):
    b, s, d = x.shape
    m = km.shape[1]
    tok = lambda w: pl.BlockSpec((1, tm, w), lambda bi, si: (bi, si, 0))
    const = lambda shape: pl.BlockSpec(shape, lambda bi, si: (0,) * len(shape),
                                       pipeline_mode=pl.Buffered(1))
    memspec = pl.BlockSpec((1, m, CROSS_WIDTH), lambda bi, si: (bi, 0, 0))
    return pl.pallas_call(
        _merge_kernel,
        grid=(b, s // tm),
        in_specs=[tok(d), const((1, d)), tok(ATT_WIDTH), tok(3 * CONV_CHANNELS), tok(CROSS_WIDTH),
                  memspec, memspec, const((d, N_BRANCHES * d)), const((1, N_BRANCHES * d)),
                  const((CONV_K, CONV_CHANNELS)), const((ATT_WIDTH, d)), const((CONV_CHANNELS, d)),
                  const((CROSS_WIDTH, d)), const((d, d))],
        out_specs=tok(d),
        out_shape=jax.ShapeDtypeStruct((b, s, d), F32),
        scratch_shapes=[pltpu.VMEM((SUBLANES, CONV_CHANNELS), F32)],
        compiler_params=pltpu.CompilerParams(
            dimension_semantics=("arbitrary", "arbitrary"), vmem_limit_bytes=VMEM_LIMIT),
        name="merge",
    )(x, g_mix, att, conv_in, qc, km, vm, wg, b_gate, conv_w, wa, wc, wm, wo)


def _ffn_kernel(x_ref, g_ref, wu_ref, cw_ref, wd_ref, gf_ref, o_ref, carry_ref, act_ref, *, fc):
    @pl.when(pl.program_id(1) == 0)
    def _():
        carry_ref[...] = jnp.zeros(carry_ref.shape, F32)

    x = x_ref[0]
    h = _rms(x, g_ref[...]).astype(BF16)
    dff = wd_ref.shape[0]
    for c0 in range(0, dff, fc):
        halves = []
        for base in (0, dff):
            cols = slice(base + c0, base + c0 + fc)
            up = _dot(h, wu_ref[:, cols])
            halves.append(_causal_conv3(up, cw_ref[:, cols], carry_ref.at[:, cols]))
        gate, val = halves
        act_ref[:, c0:c0 + fc] = (gate * jax.nn.sigmoid(gate) * val).astype(BF16)
    y = x + _dot(act_ref[...], wd_ref[...])
    o_ref[0] = _rms(y, gf_ref[...])


def _ffn(x, g_ffn, wu, conv_w, wd, g_final, tm, fc):
    b, s, d = x.shape
    dff = wd.shape[0]
    tok = pl.BlockSpec((1, tm, d), lambda bi, si: (bi, si, 0))
    const = lambda shape: pl.BlockSpec(shape, lambda bi, si: (0,) * len(shape),
                                       pipeline_mode=pl.Buffered(1))
    return pl.pallas_call(
        functools.partial(_ffn_kernel, fc=fc),
        grid=(b, s // tm),
        in_specs=[tok, const((1, d)), const((d, 2 * dff)), const((CONV_K, 2 * dff)),
                  const((dff, d)), const((1, d))],
        out_specs=tok,
        out_shape=jax.ShapeDtypeStruct((b, s, d), F32),
        scratch_shapes=[pltpu.VMEM((SUBLANES, 2 * dff), F32), pltpu.VMEM((tm, dff), BF16)],
        compiler_params=pltpu.CompilerParams(
            dimension_semantics=("arbitrary", "arbitrary"), vmem_limit_bytes=VMEM_LIMIT),
        name="ffn",
    )(x, g_ffn, wu, conv_w, wd, g_final)


def _pack_in_proj_weight(w):
    sizes = (ATT_WIDTH, KV_WIDTH, KV_WIDTH, IDX_WIDTH, IDX_DIM, N_IDX_HEADS,
             3 * CONV_CHANNELS, CROSS_WIDTH)
    offs = np.concatenate([[0], np.cumsum(sizes)])
    wq, wk, wv, wqi, wki, wwi, wconv, wqc = [w[:, offs[i]:offs[i + 1]] for i in range(len(sizes))]
    idx_scale = (N_IDX_HEADS * IDX_DIM) ** -0.5
    att_scale = HEAD_DIM ** -0.5
    wwi = jnp.pad(wwi * idx_scale, ((0, 0), (0, LANES - N_IDX_HEADS)))
    w1 = jnp.concatenate([wq * att_scale, wwi, wk, wv, wqi, jnp.tile(wki, (1, N_IDX_HEADS)),
                          wconv, wqc], axis=1)
    return w1.astype(BF16), w[:, offs[-1]:].astype(BF16)


def _layer(x, mem, g_mix, w_in, b_gate, conv_w_short, w_att_out, w_conv_out, w_mem_out, w_o,
           g_mem, w_mem_kv, g_ffn, w_up, conv_w_ffn, w_down, g_final):
    b, s, d = x.shape
    tm1 = min(512, s)
    tm4 = min(512, s)
    tm5 = min(512, s)
    w1, wg = _pack_in_proj_weight(w_in)
    q, ks, vt, qi, ki, conv_in, qc, wi = _in_proj(x, g_mix[None], w1, tm1, Q_TILE)
    km, vm = _mem_kv(mem, g_mem[None], w_mem_kv.astype(BF16))
    att = _dsa(qi, wi, q, ki, ks, vt)
    x1 = _merge(x, g_mix[None], att, conv_in, qc, km, vm, wg, b_gate[None], conv_w_short,
                w_att_out.astype(BF16), w_conv_out.astype(BF16), w_mem_out.astype(BF16),
                w_o.astype(BF16), tm4)
    dff = w_down.shape[0]
    fc = 256 if dff % 256 == 0 else dff
    return _ffn(x1, g_ffn[None], w_up.astype(BF16), conv_w_ffn, w_down.astype(BF16),
                g_final[None], tm5, fc)


def kernel(x, mem, g_mix, w_in, b_gate, conv_w_short, w_att_out, w_conv_out, w_mem_out, w_o,
           g_mem, w_mem_kv, g_ffn, w_up, conv_w_ffn, w_down, g_final):
    depth = w_in.shape[0]
    assert depth == 1, "the final RMSNorm is fused into the last layer's ffn kernel"
    return _layer(x, mem, g_mix[0], w_in[0], b_gate[0], conv_w_short[0], w_att_out[0],
                  w_conv_out[0], w_mem_out[0], w_o[0], g_mem[0], w_mem_kv[0], g_ffn[0],
                  w_up[0], conv_w_ffn[0], w_down[0], g_final)
```

```python
import functools

import numpy as np
import jax
import jax.numpy as jnp
from jax import lax
from jax.experimental import pallas as pl
from jax.experimental.pallas import tpu as pltpu

F32 = jnp.float32
BF16 = jnp.bfloat16
I32 = jnp.int32

HEAD_DIM = 64
N_ATT_HEADS = 6
N_KV_GROUPS = 2
HEADS_PER_GROUP = N_ATT_HEADS // N_KV_GROUPS
ATT_WIDTH = N_ATT_HEADS * HEAD_DIM
KV_WIDTH = N_KV_GROUPS * HEAD_DIM
N_IDX_HEADS = 8
IDX_DIM = 32
IDX_WIDTH = N_IDX_HEADS * IDX_DIM
TOPK_MAX = 256
CONV_CHANNELS = 384
CONV_K = 3
N_CROSS_HEADS = 4
CROSS_WIDTH = N_CROSS_HEADS * HEAD_DIM
N_BRANCHES = 3
ROPE_THETA = 500000.0
ROPE_FRACTION = 4
NORM_EPS = 1e-6

LANES = 128
SUBLANES = 8
VMEM_LIMIT = 56 * 1024 * 1024

Q_TILE = 256
NEG_BIG = -1e30
LOG2_E = 1.4426950408889634
INT_MIN = -2 ** 31
KEY_MIN_FINITE = -2139095040
HI16_MASK = -65536


def _rms(x, g):
    return x * lax.rsqrt(jnp.mean(x * x, axis=-1, keepdims=True) + NORM_EPS) * g


def _dot_nt(a, b):
    return lax.dot_general(a, b, (((1,), (1,)), ((), ())), preferred_element_type=F32)


def _dot(a, b):
    return jnp.dot(a, b, preferred_element_type=F32)


def _rope_tile(x, c, sa, sb, half):
    return x * c + pltpu.roll(x, LANES - half, 1) * sa + pltpu.roll(x, half, 1) * sb


def _rope(x, c, sa, sb, half):
    tiles = [_rope_tile(x[:, i:i + LANES], c, sa, sb, half) for i in range(0, x.shape[1], LANES)]
    return tiles[0] if len(tiles) == 1 else jnp.concatenate(tiles, axis=1)


_OFF_Q = 0
_OFF_WI = _OFF_Q + ATT_WIDTH
_OFF_KV = _OFF_WI + LANES
_OFF_QI = _OFF_KV + 2 * KV_WIDTH
_OFF_KI = _OFF_QI + IDX_WIDTH
_OFF_CONV = _OFF_KI + IDX_WIDTH
_OFF_QC = _OFF_CONV + 3 * CONV_CHANNELS
_W1_WIDTH = _OFF_QC + CROSS_WIDTH


def _in_proj_kernel(x_ref, g_ref, w_ref, ca_ref, saa_ref, sba_ref, ci_ref, sai_ref, sbi_ref,
                    q_ref, ks_ref, vt_ref, qi_ref, ki_ref, conv_ref, qc_ref, wi_ref, *, kchunk):
    h = _rms(x_ref[0], g_ref[...]).astype(BF16)
    ca, saa, sba = ca_ref[...], saa_ref[...], sba_ref[...]
    ci, sai, sbi = ci_ref[...], sai_ref[...], sbi_ref[...]
    half_a = HEAD_DIM // ROPE_FRACTION // 2
    half_i = IDX_DIM // ROPE_FRACTION // 2

    a = _dot(h, w_ref[:, _OFF_Q:_OFF_KV])
    q_ref[0] = (_rope(a[:, :ATT_WIDTH], ca, saa, sba, half_a) * LOG2_E).astype(BF16)
    wi_ref[0] = a[:, ATT_WIDTH:]

    kv = _dot(h, w_ref[:, _OFF_KV:_OFF_QI])
    ks_ref[0] = _rope(kv[:, :KV_WIDTH], ca, saa, sba, half_a).astype(BF16)
    v = kv[:, KV_WIDTH:]
    for i in range(v.shape[0] // kchunk):
        vt_ref[0, i] = v[i * kchunk:(i + 1) * kchunk].T.astype(BF16)

    idx = _dot(h, w_ref[:, _OFF_QI:_OFF_CONV])
    qi_ref[0] = _rope(idx[:, :IDX_WIDTH], ci, sai, sbi, half_i).astype(BF16)
    ki_ref[0] = _rope(idx[:, IDX_WIDTH:], ci, sai, sbi, half_i).astype(BF16)

    cq = _dot(h, w_ref[:, _OFF_CONV:_W1_WIDTH])
    conv_ref[0] = cq[:, :3 * CONV_CHANNELS].astype(BF16)
    qc_ref[0] = cq[:, 3 * CONV_CHANNELS:].astype(BF16)


def _rope_tables(s, period, rot_dim):
    half = rot_dim // 2
    inv_freq = ROPE_THETA ** (-jnp.arange(half, dtype=F32) / half)
    ang = jnp.arange(s).astype(F32)[:, None] * inv_freq[None, :]
    cos, sin = jnp.cos(ang), jnp.sin(ang)
    j = np.arange(LANES) % period
    first = jnp.asarray(j < half)[None, :]
    second = jnp.asarray((j >= half) & (j < rot_dim))[None, :]
    fidx = np.where(j < half, j, np.where(j < rot_dim, j - half, 0))
    cos_l, sin_l = cos[:, fidx], sin[:, fidx]
    c = jnp.where(first | second, cos_l, 1.0)
    sa = jnp.where(first, -sin_l, 0.0)
    sb = jnp.where(second, sin_l, 0.0)
    return c, sa, sb


def _in_proj(x, g_mix, w1, tm, kchunk):
    b, s, d = x.shape
    tabs = _rope_tables(s, HEAD_DIM, HEAD_DIM // ROPE_FRACTION) + \
        _rope_tables(s, IDX_DIM, IDX_DIM // ROPE_FRACTION)
    tok = lambda w: pl.BlockSpec((1, tm, w), lambda si, bi: (bi, si, 0))
    tab = pl.BlockSpec((tm, LANES), lambda si, bi: (si, 0))
    const = lambda shape: pl.BlockSpec(shape, lambda si, bi: (0,) * len(shape),
                                       pipeline_mode=pl.Buffered(1))
    out_shape = (
        jax.ShapeDtypeStruct((b, s, ATT_WIDTH), BF16),
        jax.ShapeDtypeStruct((b, s, KV_WIDTH), BF16),
        jax.ShapeDtypeStruct((b, s // kchunk, KV_WIDTH, kchunk), BF16),
        jax.ShapeDtypeStruct((b, s, IDX_WIDTH), BF16),
        jax.ShapeDtypeStruct((b, s, IDX_WIDTH), BF16),
        jax.ShapeDtypeStruct((b, s, 3 * CONV_CHANNELS), BF16),
        jax.ShapeDtypeStruct((b, s, CROSS_WIDTH), BF16),
        jax.ShapeDtypeStruct((b, s, LANES), F32),
    )
    out_specs = (tok(ATT_WIDTH), tok(KV_WIDTH),
                 pl.BlockSpec((1, tm // kchunk, KV_WIDTH, kchunk), lambda si, bi: (bi, si, 0, 0)),
                 tok(IDX_WIDTH), tok(IDX_WIDTH), tok(3 * CONV_CHANNELS), tok(CROSS_WIDTH), tok(LANES))
    return pl.pallas_call(
        functools.partial(_in_proj_kernel, kchunk=kchunk),
        grid=(s // tm, b),
        in_specs=[tok(d), const((1, d)), const((d, _W1_WIDTH))] + [tab] * 6,
        out_specs=out_specs,
        out_shape=out_shape,
        compiler_params=pltpu.CompilerParams(
            dimension_semantics=("arbitrary", "arbitrary"), vmem_limit_bytes=VMEM_LIMIT),
        name="in_proj",
    )(x, g_mix, w1, *tabs)


def _mem_kv_kernel(mem_ref, g_ref, w_ref, km_ref, vm_ref):
    h = _rms(mem_ref[0], g_ref[...]).astype(BF16)
    kv = _dot(h, w_ref[...])
    km_ref[0] = kv[:, :CROSS_WIDTH].astype(BF16)
    vm_ref[0] = kv[:, CROSS_WIDTH:].astype(BF16)


def _mem_kv(mem, g_mem, w_kv):
    b, m, d = mem.shape
    const = lambda shape: pl.BlockSpec(shape, lambda bi: (0,) * len(shape))
    spec = pl.BlockSpec((1, m, CROSS_WIDTH), lambda bi: (bi, 0, 0))
    return pl.pallas_call(
        _mem_kv_kernel,
        grid=(b,),
        in_specs=[pl.BlockSpec((1, m, d), lambda bi: (bi, 0, 0)), const((1, d)),
                  const((d, 2 * CROSS_WIDTH))],
        out_specs=(spec, spec),
        out_shape=(jax.ShapeDtypeStruct((b, m, CROSS_WIDTH), BF16),) * 2,
        compiler_params=pltpu.CompilerParams(dimension_semantics=("arbitrary",)),
        name="mem_kv",
    )(mem, g_mem, w_kv)


def _key_to_float(key):
    return lax.bitcast_convert_type(key ^ ((key >> 31) & 0x7FFFFFFF), F32)


def _for_chunks(n, body, init, slab=False):
    def pair(i, carry):
        if slab:
            return body(2 * i, carry, 2)
        return body(2 * i + 1, body(2 * i, carry))
    carry = lax.fori_loop(0, n // 2, pair, init)
    return lax.cond(n % 2 == 1, lambda cr: body(n - 1, cr), lambda cr: cr, carry)


def _dsa_kernel(qi_ref, wi_ref, q_ref, ki_ref, ks_ref, vt_ref, o_ref, sc_ref, sh_ref, acc_ref,
                *, topk, jbits):
    tq = ch = Q_TILE
    qb = pl.program_id(1)
    nc = qb + 1
    qpos = qb * tq + lax.broadcasted_iota(I32, (1, tq), 1)

    def chunk_pos(c):
        off = pl.multiple_of(c * ch, ch)
        return off, off + lax.broadcasted_iota(I32, (ch, tq), 0)

    qi = qi_ref[0]
    head_of_lane = lax.broadcasted_iota(I32, (tq, IDX_WIDTH), 1) // IDX_DIM
    zero_bf = jnp.zeros((), BF16)
    qi_heads = [jnp.where(head_of_lane == h, qi, zero_bf) for h in range(N_IDX_HEADS)]
    wit = wi_ref[0].T
    wrow = [wit[h:h + 1, :] for h in range(N_IDX_HEADS)]

    def scores(c, nchunks=1):
        off = pl.multiple_of(c * ch, ch)
        kc = ki_ref[0, pl.ds(off, nchunks * ch), :]
        acc = jnp.maximum(_dot_nt(kc, qi_heads[0]), 0.0) * wrow[0]
        for h in range(1, N_IDX_HEADS):
            acc = acc + jnp.maximum(_dot_nt(kc, qi_heads[h]), 0.0) * wrow[h]
        return off, acc

    def put_scores(off, s):
        sc_ref[pl.ds(off, s.shape[0]), :] = s
        hi = lax.bitcast_convert_type(lax.bitcast_convert_type(s, I32) & HI16_MASK, F32)
        sh_ref[pl.ds(off, s.shape[0]), :] = hi.astype(BF16)

    def score_chunks(c, carry, nchunks=1):
        put_scores(*scores(c, nchunks))
        return carry

    _for_chunks(qb, score_chunks, 0, slab=True)
    off, acc = scores(qb)
    put_scores(off, jnp.where(chunk_pos(qb)[1] <= qpos, acc, -jnp.inf))

    rows16 = 2 * SUBLANES

    def count_hi(cand):
        def body(c, cnt):
            off, _ = chunk_pos(c)
            hit = jnp.where(sh_ref[pl.ds(off, ch), :] >= cand, jnp.ones((), BF16), zero_bf)
            parts = [hit[r:r + rows16] for r in range(0, ch, rows16)]
            while len(parts) > 1:
                parts = [a + b for a, b in zip(parts[::2], parts[1::2])]
            return cnt + parts[0].astype(F32)
        cnt = _for_chunks(nc, body, jnp.zeros((rows16, tq), F32))
        return cnt.sum(axis=0, keepdims=True)

    def key_hi_bit(i, key):
        trial = key + lax.shift_left(jnp.int32(1), 31 - i)
        pattern = trial ^ ((trial >> 31) & 0x7FFF0000)
        cand = lax.bitcast_convert_type(pattern, F32).astype(BF16)
        return jnp.where(count_hi(cand) >= topk, trial, key)

    def count(pred):
        def body(c, cnt):
            off, kpos = chunk_pos(c)
            hit = jnp.where(pred(sc_ref[pl.ds(off, ch), :], kpos), 1.0, 0.0)
            return cnt + hit.reshape(ch // SUBLANES, SUBLANES, tq).sum(axis=0)
        cnt = _for_chunks(nc, body, jnp.zeros((SUBLANES, tq), F32))
        return cnt.sum(axis=0, keepdims=True)

    def key_bit(i, key):
        trial = key + lax.shift_left(jnp.int32(1), 31 - i)
        cand = _key_to_float(trial)
        return jnp.where(count(lambda s, _: s >= cand) >= topk, trial, key)

    key = lax.fori_loop(0, 16, key_hi_bit, jnp.full((1, tq), INT_MIN, I32))
    key = lax.fori_loop(16, 32, key_bit, key)
    tau = _key_to_float(jnp.maximum(key, KEY_MIN_FINITE))

    n_ge = count(lambda s, _: s >= tau)
    any_excess = jnp.max(jnp.where(n_ge > topk, 1, 0)) > 0

    @pl.when(any_excess)
    def _():
        want = topk - count(lambda s, _: s > tau)

        def pos_bit(i, jmax):
            trial = jmax + lax.shift_left(jnp.int32(1), jbits - 1 - i)
            n_before = count(lambda s, kpos: jnp.where(s == tau, kpos, trial) < trial)
            return jnp.where(n_before <= want, trial, jmax)

        jmax = lax.fori_loop(0, jbits, pos_bit, jnp.zeros((1, tq), I32))

        def drop(c, carry):
            off, kpos = chunk_pos(c)
            s = sc_ref[pl.ds(off, ch), :]
            late_tie = jnp.where(s == tau, kpos, -1) >= jmax
            sc_ref[pl.ds(off, ch), :] = jnp.where(late_tie, -jnp.inf, s)
            return carry

        lax.fori_loop(0, nc, drop, 0)

    q = q_ref[0]
    group_of_lane = lax.broadcasted_iota(I32, (tq, LANES), 1) // HEAD_DIM
    qe = []
    for h in range(N_ATT_HEADS):
        g = h // HEADS_PER_GROUP
        t = q[:, (h // 2) * LANES:(h // 2 + 1) * LANES].astype(F32)
        if h % 2 != g:
            t = pltpu.roll(t, HEAD_DIM, 1)
        qe.append(jnp.where(group_of_lane == g, t, 0.0).astype(BF16))

    acc_ref[...] = jnp.zeros(acc_ref.shape, F32)

    def attend(c, carry, nchunks=1):
        ms, ls = carry
        off = pl.multiple_of(c * ch, ch)
        rows = nchunks * ch
        kc = ks_ref[0, pl.ds(off, rows), :]
        vt = vt_ref[0, c]
        if nchunks == 2:
            vt = jnp.concatenate([vt, vt_ref[0, c + 1]], axis=1)
        bias = jnp.where(sc_ref[pl.ds(off, rows), :] >= tau, 0.0, NEG_BIG)
        new_ms, new_ls = [], []
        lgs = [_dot_nt(kc, qe[h]) for h in range(N_ATT_HEADS)]
        for h in range(N_ATT_HEADS):
            lg = bias + lgs[h]
            m_new = jnp.maximum(ms[h], lg.max(axis=0, keepdims=True))
            alpha = jnp.exp2(ms[h] - m_new)
            p = jnp.exp2(lg - m_new)
            new_ms.append(m_new)
            new_ls.append(alpha * ls[h] + p.sum(axis=0, keepdims=True))
            cols = slice(h * tq, (h + 1) * tq)
            acc_ref[:, cols] = alpha * acc_ref[:, cols] + _dot(vt, p.astype(BF16))
        return tuple(new_ms), tuple(new_ls)

    init = (tuple(jnp.full((1, tq), NEG_BIG, F32) for _ in range(N_ATT_HEADS)),
            tuple(jnp.zeros((1, tq), F32) for _ in range(N_ATT_HEADS)))
    _, ls = _for_chunks(nc, attend, init, slab=True)

    lane = lax.broadcasted_iota(I32, (tq, LANES), 1)
    tiles = []
    for j in range(N_ATT_HEADS // 2):
        parts = []
        for p_ in range(2):
            h = 2 * j + p_
            blk = acc_ref[:, h * tq:(h + 1) * tq] * (1.0 / ls[h])
            if h // HEADS_PER_GROUP != p_:
                blk = jnp.concatenate([blk[HEAD_DIM:], blk[:HEAD_DIM]], axis=0)
            parts.append(blk.T)
        tiles.append(jnp.where(lane < HEAD_DIM, parts[0], parts[1]))
    o_ref[0] = jnp.concatenate(tiles, axis=1).astype(BF16)


def _dsa(qi, wi, q, ki, ks, vt):
    b, s, _ = q.shape
    topk = min(TOPK_MAX, s // 4)
    blk = lambda w: pl.BlockSpec((1, Q_TILE, w), lambda bi, qb: (bi, qb, 0))
    full = lambda w: pl.BlockSpec((1, s, w), lambda bi, qb: (bi, 0, 0))
    return pl.pallas_call(
        functools.partial(_dsa_kernel, topk=topk, jbits=int(s).bit_length()),
        grid=(b, s // Q_TILE),
        in_specs=[blk(IDX_WIDTH), blk(LANES), blk(ATT_WIDTH), full(IDX_WIDTH), full(KV_WIDTH),
                  pl.BlockSpec((1, s // Q_TILE, KV_WIDTH, Q_TILE), lambda bi, qb: (bi, 0, 0, 0))],
        out_specs=blk(ATT_WIDTH),
        out_shape=jax.ShapeDtypeStruct((b, s, ATT_WIDTH), BF16),
        scratch_shapes=[pltpu.VMEM((s, Q_TILE), F32), pltpu.VMEM((s, Q_TILE), BF16),
                        pltpu.VMEM((KV_WIDTH, N_ATT_HEADS * Q_TILE), F32)],
        compiler_params=pltpu.CompilerParams(
            dimension_semantics=("arbitrary", "arbitrary"), vmem_limit_bytes=VMEM_LIMIT),
        name="dsa",
    )(qi, wi, q, ki, ks, vt)


def _causal_conv3(u, w, carry_ref):
    prev = carry_ref[...]
    row = lax.broadcasted_iota(I32, prev.shape, 0)
    s1 = pltpu.roll(u, 1, 0)
    s2 = pltpu.roll(u, 2, 0)
    p1 = pltpu.roll(prev, 1, 0)
    p2 = pltpu.roll(prev, 2, 0)
    s1 = jnp.concatenate([jnp.where(row < 1, p1, s1[:SUBLANES]), s1[SUBLANES:]], axis=0)
    s2 = jnp.concatenate([jnp.where(row < 2, p2, s2[:SUBLANES]), s2[SUBLANES:]], axis=0)
    carry_ref[...] = u[u.shape[0] - SUBLANES:]
    return s2 * w[0:1] + s1 * w[1:2] + u * w[2:3]


def _merge_kernel(x_ref, g_ref, att_ref, conv_ref, qc_ref, km_ref, vm_ref, wg_ref, bg_ref, cw_ref,
                  wa_ref, wc_ref, wm_ref, wo_ref, o_ref, carry_ref):
    @pl.when(pl.program_id(1) == 0)
    def _():
        carry_ref[...] = jnp.zeros(carry_ref.shape, F32)

    x = x_ref[0]
    h = _rms(x, g_ref[...]).astype(BF16)

    y_att = _dot(att_ref[0], wa_ref[...])

    cin = conv_ref[0].astype(F32)
    c = CONV_CHANNELS
    cu = cin[:, c:2 * c] * cin[:, 2 * c:]
    yc = cin[:, :c] * _causal_conv3(cu, cw_ref[...], carry_ref)
    y_conv = _dot(yc.astype(BF16), wc_ref[...])

    qc, km, vm = qc_ref[0], km_ref[0], vm_ref[0]
    head_of_lane = lax.broadcasted_iota(I32, km.shape, 1) // HEAD_DIM
    zero_bf = jnp.zeros((), BF16)
    mem_out = jnp.zeros((x.shape[0], CROSS_WIDTH), F32)
    for hh in range(N_CROSS_HEADS):
        lg = _dot_nt(qc, jnp.where(head_of_lane == hh, km, zero_bf)) * (HEAD_DIM ** -0.5)
        e = jnp.exp(lg - lg.max(axis=-1, keepdims=True))
        p = e / e.sum(axis=-1, keepdims=True)
        mem_out = mem_out + _dot(p.astype(BF16), jnp.where(head_of_lane == hh, vm, zero_bf))
    y_mem = _dot(mem_out.astype(BF16), wm_ref[...])

    d = x.shape[1]
    merged = jnp.zeros(x.shape, F32)
    for i, y in enumerate((y_att, y_conv, y_mem)):
        pre = _dot(h, wg_ref[:, i * d:(i + 1) * d]) + bg_ref[:, i * d:(i + 1) * d]
        merged = merged + jax.nn.sigmoid(pre) * y
    o_ref[0] = x + _dot(merged.astype(BF16), wo_ref[...])


def _merge(x, g_mix, att, conv_in, qc, km, vm, wg, b_gate, conv_w, wa, wc, wm, wo, tm):
    b, s, d = x.shape
    m = km.shape[1]
    tok = lambda w: pl.BlockSpec((1, tm, w), lambda bi, si: (bi, si, 0))
    const = lambda shape: pl.BlockSpec(shape, lambda bi, si: (0,) * len(shape),
                                       pipeline_mode=pl.Buffered(1))
    memspec = pl.BlockSpec((1, m, CROSS_WIDTH), lambda bi, si: (bi, 0, 0))
    return pl.pallas_call(
        _merge_kernel,
        grid=(b, s // tm),
        in_specs=[tok(d), const((1, d)), tok(ATT_WIDTH), tok(3 * CONV_CHANNELS), tok(CROSS_WIDTH),
                  memspec, memspec, const((d, N_BRANCHES * d)), const((1, N_BRANCHES * d)),
                  const((CONV_K, CONV_CHANNELS)), const((ATT_WIDTH, d)), const((CONV_CHANNELS, d)),
                  const((CROSS_WIDTH, d)), const((d, d))],
        out_specs=tok(d),
        out_shape=jax.ShapeDtypeStruct((b, s, d), F32),
        scratch_shapes=[pltpu.VMEM((SUBLANES, CONV_CHANNELS), F32)],
        compiler_params=pltpu.CompilerParams(
            dimension_semantics=("arbitrary", "arbitrary"), vmem_limit_bytes=VMEM_LIMIT),
        name="merge",
    )(x, g_mix, att, conv_in, qc, km, vm, wg, b_gate, conv_w, wa, wc, wm, wo)


def _ffn_kernel(x_ref, g_ref, wu_ref, cw_ref, wd_ref, gf_ref, o_ref, carry_ref, act_ref, *, fc):
    @pl.when(pl.program_id(1) == 0)
    def _():
        carry_ref[...] = jnp.zeros(carry_ref.shape, F32)

    x = x_ref[0]
    h = _rms(x, g_ref[...]).astype(BF16)
    dff = wd_ref.shape[0]
    for c0 in range(0, dff, fc):
        halves = []
        for base in (0, dff):
            cols = slice(base + c0, base + c0 + fc)
            up = _dot(h, wu_ref[:, cols])
            halves.append(_causal_conv3(up, cw_ref[:, cols], carry_ref.at[:, cols]))
        gate, val = halves
        act_ref[:, c0:c0 + fc] = (gate * jax.nn.sigmoid(gate) * val).astype(BF16)
    y = x + _dot(act_ref[...], wd_ref[...])
    o_ref[0] = _rms(y, gf_ref[...])


def _ffn(x, g_ffn, wu, conv_w, wd, g_final, tm, fc):
    b, s, d = x.shape
    dff = wd.shape[0]
    tok = pl.BlockSpec((1, tm, d), lambda bi, si: (bi, si, 0))
    const = lambda shape: pl.BlockSpec(shape, lambda bi, si: (0,) * len(shape),
                                       pipeline_mode=pl.Buffered(1))
    return pl.pallas_call(
        functools.partial(_ffn_kernel, fc=fc),
        grid=(b, s // tm),
        in_specs=[tok, const((1, d)), const((d, 2 * dff)), const((CONV_K, 2 * dff)),
                  const((dff, d)), const((1, d))],
        out_specs=tok,
        out_shape=jax.ShapeDtypeStruct((b, s, d), F32),
        scratch_shapes=[pltpu.VMEM((SUBLANES, 2 * dff), F32), pltpu.VMEM((tm, dff), BF16)],
        compiler_params=pltpu.CompilerParams(
            dimension_semantics=("arbitrary", "arbitrary"), vmem_limit_bytes=VMEM_LIMIT),
        name="ffn",
    )(x, g_ffn, wu, conv_w, wd, g_final)


def _pack_in_proj_weight(w):
    sizes = (ATT_WIDTH, KV_WIDTH, KV_WIDTH, IDX_WIDTH, IDX_DIM, N_IDX_HEADS,
             3 * CONV_CHANNELS, CROSS_WIDTH)
    offs = np.concatenate([[0], np.cumsum(sizes)])
    wq, wk, wv, wqi, wki, wwi, wconv, wqc = [w[:, offs[i]:offs[i + 1]] for i in range(len(sizes))]
    idx_scale = (N_IDX_HEADS * IDX_DIM) ** -0.5
    att_scale = HEAD_DIM ** -0.5
    wwi = jnp.pad(wwi * idx_scale, ((0, 0), (0, LANES - N_IDX_HEADS)))
    w1 = jnp.concatenate([wq * att_scale, wwi, wk, wv, wqi, jnp.tile(wki, (1, N_IDX_HEADS)),
                          wconv, wqc], axis=1)
    return w1.astype(BF16), w[:, offs[-1]:].astype(BF16)


def _layer(x, mem, g_mix, w_in, b_gate, conv_w_short, w_att_out, w_conv_out, w_mem_out, w_o,
           g_mem, w_mem_kv, g_ffn, w_up, conv_w_ffn, w_down, g_final):
    b, s, d = x.shape
    tm1 = min(512, s)
    tm4 = min(512, s)
    tm5 = min(512, s)
    w1, wg = _pack_in_proj_weight(w_in)
    q, ks, vt, qi, ki, conv_in, qc, wi = _in_proj(x, g_mix[None], w1, tm1, Q_TILE)
    km, vm = _mem_kv(mem, g_mem[None], w_mem_kv.astype(BF16))
    att = _dsa(qi, wi, q, ki, ks, vt)
    x1 = _merge(x, g_mix[None], att, conv_in, qc, km, vm, wg, b_gate[None], conv_w_short,
                w_att_out.astype(BF16), w_conv_out.astype(BF16), w_mem_out.astype(BF16),
                w_o.astype(BF16), tm4)
    dff = w_down.shape[0]
    fc = 256 if dff % 256 == 0 else dff
    return _ffn(x1, g_ffn[None], w_up.astype(BF16), conv_w_ffn, w_down.astype(BF16),
                g_final[None], tm5, fc)


def kernel(x, mem, g_mix, w_in, b_gate, conv_w_short, w_att_out, w_conv_out, w_mem_out, w_o,
           g_mem, w_mem_kv, g_ffn, w_up, conv_w_ffn, w_down, g_final):
    depth = w_in.shape[0]
    assert depth == 1, "the final RMSNorm is fused into the last layer's ffn kernel"
    return _layer(x, mem, g_mix[0], w_in[0], b_gate[0], conv_w_short[0], w_att_out[0],
                  w_conv_out[0], w_mem_out[0], w_o[0], g_mem[0], w_mem_kv[0], g_ffn[0],
                  w_up[0], conv_w_ffn[0], w_down[0], g_final)
```

```python
import functools

import numpy as np
import jax
import jax.numpy as jnp
from jax import lax
from jax.experimental import pallas as pl
from jax.experimental.pallas import tpu as pltpu

F32 = jnp.float32
BF16 = jnp.bfloat16
I32 = jnp.int32

HEAD_DIM = 64
N_ATT_HEADS = 6
N_KV_GROUPS = 2
HEADS_PER_GROUP = N_ATT_HEADS // N_KV_GROUPS
ATT_WIDTH = N_ATT_HEADS * HEAD_DIM
KV_WIDTH = N_KV_GROUPS * HEAD_DIM
N_IDX_HEADS = 8
IDX_DIM = 32
IDX_WIDTH = N_IDX_HEADS * IDX_DIM
TOPK_MAX = 256
CONV_CHANNELS = 384
CONV_K = 3
N_CROSS_HEADS = 4
CROSS_WIDTH = N_CROSS_HEADS * HEAD_DIM
N_BRANCHES = 3
ROPE_THETA = 500000.0
ROPE_FRACTION = 4
NORM_EPS = 1e-6

LANES = 128
SUBLANES = 8
VMEM_LIMIT = 56 * 1024 * 1024

Q_TILE = 256
NEG_BIG = -1e30
LOG2_E = 1.4426950408889634
INT_MIN = -2 ** 31
KEY_MIN_FINITE = -2139095040
HI16_MASK = -65536


def _rms(x, g):
    return x * lax.rsqrt(jnp.mean(x * x, axis=-1, keepdims=True) + NORM_EPS) * g


def _dot_nt(a, b):
    return lax.dot_general(a, b, (((1,), (1,)), ((), ())), preferred_element_type=F32)


def _dot(a, b):
    return jnp.dot(a, b, preferred_element_type=F32)


def _rope_tile(x, c, sa, sb, half):
    return x * c + pltpu.roll(x, LANES - half, 1) * sa + pltpu.roll(x, half, 1) * sb


def _rope(x, c, sa, sb, half):
    tiles = [_rope_tile(x[:, i:i + LANES], c, sa, sb, half) for i in range(0, x.shape[1], LANES)]
    return tiles[0] if len(tiles) == 1 else jnp.concatenate(tiles, axis=1)


_OFF_Q = 0
_OFF_WI = _OFF_Q + ATT_WIDTH
_OFF_KV = _OFF_WI + LANES
_OFF_QI = _OFF_KV + 2 * KV_WIDTH
_OFF_KI = _OFF_QI + IDX_WIDTH
_OFF_CONV = _OFF_KI + IDX_WIDTH
_OFF_QC = _OFF_CONV + 3 * CONV_CHANNELS
_W1_WIDTH = _OFF_QC + CROSS_WIDTH


def _in_proj_kernel(x_ref, g_ref, w_ref, ca_ref, saa_ref, sba_ref, ci_ref, sai_ref, sbi_ref,
                    q_ref, ks_ref, vt_ref, qi_ref, ki_ref, conv_ref, qc_ref, wi_ref, *, kchunk):
    h = _rms(x_ref[0], g_ref[...]).astype(BF16)
    ca, saa, sba = ca_ref[...], saa_ref[...], sba_ref[...]
    ci, sai, sbi = ci_ref[...], sai_ref[...], sbi_ref[...]
    half_a = HEAD_DIM // ROPE_FRACTION // 2
    half_i = IDX_DIM // ROPE_FRACTION // 2

    a = _dot(h, w_ref[:, _OFF_Q:_OFF_KV])
    q_ref[0] = (_rope(a[:, :ATT_WIDTH], ca, saa, sba, half_a) * LOG2_E).astype(BF16)
    wi_ref[0] = a[:, ATT_WIDTH:]

    kv = _dot(h, w_ref[:, _OFF_KV:_OFF_QI])
    ks_ref[0] = _rope(kv[:, :KV_WIDTH], ca, saa, sba, half_a).astype(BF16)
    v = kv[:, KV_WIDTH:]
    for i in range(v.shape[0] // kchunk):
        vt_ref[0, i] = v[i * kchunk:(i + 1) * kchunk].T.astype(BF16)

    idx = _dot(h, w_ref[:, _OFF_QI:_OFF_CONV])
    qi_ref[0] = _rope(idx[:, :IDX_WIDTH], ci, sai, sbi, half_i).astype(BF16)
    ki_ref[0] = _rope(idx[:, IDX_WIDTH:], ci, sai, sbi, half_i).astype(BF16)

    cq = _dot(h, w_ref[:, _OFF_CONV:_W1_WIDTH])
    conv_ref[0] = cq[:, :3 * CONV_CHANNELS].astype(BF16)
    qc_ref[0] = cq[:, 3 * CONV_CHANNELS:].astype(BF16)


def _rope_tables(s, period, rot_dim):
    half = rot_dim // 2
    inv_freq = ROPE_THETA ** (-jnp.arange(half, dtype=F32) / half)
    ang = jnp.arange(s).astype(F32)[:, None] * inv_freq[None, :]
    cos, sin = jnp.cos(ang), jnp.sin(ang)
    j = np.arange(LANES) % period
    first = jnp.asarray(j < half)[None, :]
    second = jnp.asarray((j >= half) & (j < rot_dim))[None, :]
    fidx = np.where(j < half, j, np.where(j < rot_dim, j - half, 0))
    cos_l, sin_l = cos[:, fidx], sin[:, fidx]
    c = jnp.where(first | second, cos_l, 1.0)
    sa = jnp.where(first, -sin_l, 0.0)
    sb = jnp.where(second, sin_l, 0.0)
    return c, sa, sb


def _in_proj(x, g_mix, w1, tm, kchunk):
    b, s, d = x.shape
    tabs = _rope_tables(s, HEAD_DIM, HEAD_DIM // ROPE_FRACTION) + \
        _rope_tables(s, IDX_DIM, IDX_DIM // ROPE_FRACTION)
    tok = lambda w: pl.BlockSpec((1, tm, w), lambda si, bi: (bi, si, 0))
    tab = pl.BlockSpec((tm, LANES), lambda si, bi: (si, 0))
    const = lambda shape: pl.BlockSpec(shape, lambda si, bi: (0,) * len(shape),
                                       pipeline_mode=pl.Buffered(1))
    out_shape = (
        jax.ShapeDtypeStruct((b, s, ATT_WIDTH), BF16),
        jax.ShapeDtypeStruct((b, s, KV_WIDTH), BF16),
        jax.ShapeDtypeStruct((b, s // kchunk, KV_WIDTH, kchunk), BF16),
        jax.ShapeDtypeStruct((b, s, IDX_WIDTH), BF16),
        jax.ShapeDtypeStruct((b, s, IDX_WIDTH), BF16),
        jax.ShapeDtypeStruct((b, s, 3 * CONV_CHANNELS), BF16),
        jax.ShapeDtypeStruct((b, s, CROSS_WIDTH), BF16),
        jax.ShapeDtypeStruct((b, s, LANES), F32),
    )
    out_specs = (tok(ATT_WIDTH), tok(KV_WIDTH),
                 pl.BlockSpec((1, tm // kchunk, KV_WIDTH, kchunk), lambda si, bi: (bi, si, 0, 0)),
                 tok(IDX_WIDTH), tok(IDX_WIDTH), tok(3 * CONV_CHANNELS), tok(CROSS_WIDTH), tok(LANES))
    return pl.pallas_call(
        functools.partial(_in_proj_kernel, kchunk=kchunk),
        grid=(s // tm, b),
        in_specs=[tok(d), const((1, d)), const((d, _W1_WIDTH))] + [tab] * 6,
        out_specs=out_specs,
        out_shape=out_shape,
        compiler_params=pltpu.CompilerParams(
            dimension_semantics=("arbitrary", "arbitrary"), vmem_limit_bytes=VMEM_LIMIT),
        name="in_proj",
    )(x, g_mix, w1, *tabs)


def _mem_kv_kernel(mem_ref, g_ref, w_ref, km_ref, vm_ref):
    h = _rms(mem_ref[0], g_ref[...]).astype(BF16)
    kv = _dot(h, w_ref[...])
    km_ref[0] = kv[:, :CROSS_WIDTH].astype(BF16)
    vm_ref[0] = kv[:, CROSS_WIDTH:].astype(BF16)


def _mem_kv(mem, g_mem, w_kv):
    b, m, d = mem.shape
    const = lambda shape: pl.BlockSpec(shape, lambda bi: (0,) * len(shape))
    spec = pl.BlockSpec((1, m, CROSS_WIDTH), lambda bi: (bi, 0, 0))
    return pl.pallas_call(
        _mem_kv_kernel,
        grid=(b,),
        in_specs=[pl.BlockSpec((1, m, d), lambda bi: (bi, 0, 0)), const((1, d)),
                  const((d, 2 * CROSS_WIDTH))],
        out_specs=(spec, spec),
        out_shape=(jax.ShapeDtypeStruct((b, m, CROSS_WIDTH), BF16),) * 2,
        compiler_params=pltpu.CompilerParams(dimension_semantics=("arbitrary",)),
        name="mem_kv",
    )(mem, g_mem, w_kv)


def _key_to_float(key):
    return lax.bitcast_convert_type(key ^ ((key >> 31) & 0x7FFFFFFF), F32)


def _col_reduce(x, op):
    rows, n = x.shape
    r = op(x.reshape(SUBLANES, rows // (SUBLANES * SUBLANES), SUBLANES, n), axis=1)
    return op(op(r, axis=0), axis=0, keepdims=True)


def _for_chunks(n, body, init, slab=False):
    def pair(i, carry):
        if slab:
            return body(2 * i, carry, 2)
        return body(2 * i + 1, body(2 * i, carry))
    carry = lax.fori_loop(0, n // 2, pair, init)
    return lax.cond(n % 2 == 1, lambda cr: body(n - 1, cr), lambda cr: cr, carry)


def _dsa_kernel(qi_ref, wi_ref, q_ref, ki_ref, ks_ref, vt_ref, o_ref, sc_ref, sh_ref, acc_ref,
                *, topk, jbits):
    tq = ch = Q_TILE
    qb = pl.program_id(1)
    nc = qb + 1
    qpos = qb * tq + lax.broadcasted_iota(I32, (1, tq), 1)

    def chunk_pos(c):
        off = pl.multiple_of(c * ch, ch)
        return off, off + lax.broadcasted_iota(I32, (ch, tq), 0)

    qi = qi_ref[0]
    head_of_lane = lax.broadcasted_iota(I32, (tq, IDX_WIDTH), 1) // IDX_DIM
    zero_bf = jnp.zeros((), BF16)
    qi_heads = [jnp.where(head_of_lane == h, qi, zero_bf) for h in range(N_IDX_HEADS)]
    wit = wi_ref[0].T
    wrow = [wit[h:h + 1, :] for h in range(N_IDX_HEADS)]

    def scores(c, nchunks=1):
        off = pl.multiple_of(c * ch, ch)
        kc = ki_ref[0, pl.ds(off, nchunks * ch), :]
        acc = jnp.maximum(_dot_nt(kc, qi_heads[0]), 0.0) * wrow[0]
        for h in range(1, N_IDX_HEADS):
            acc = acc + jnp.maximum(_dot_nt(kc, qi_heads[h]), 0.0) * wrow[h]
        return off, acc

    def put_scores(off, s):
        sc_ref[pl.ds(off, s.shape[0]), :] = s
        hi = lax.bitcast_convert_type(lax.bitcast_convert_type(s, I32) & HI16_MASK, F32)
        sh_ref[pl.ds(off, s.shape[0]), :] = hi.astype(BF16)

    def score_chunks(c, carry, nchunks=1):
        put_scores(*scores(c, nchunks))
        return carry

    _for_chunks(qb, score_chunks, 0, slab=True)
    off, acc = scores(qb)
    put_scores(off, jnp.where(chunk_pos(qb)[1] <= qpos, acc, -jnp.inf))

    def count_ge(ref, cands):
        rows = SUBLANES * (4 // ref.dtype.itemsize)
        one, zero = jnp.ones((), ref.dtype), jnp.zeros((), ref.dtype)

        def body(c, cnts):
            off, _ = chunk_pos(c)
            s = ref[pl.ds(off, ch), :]
            out = []
            for cand, cnt in zip(cands, cnts):
                hit = jnp.where(s >= cand, one, zero)
                parts = [hit[r:r + rows] for r in range(0, ch, rows)]
                while len(parts) > 1:
                    parts = [a + b for a, b in zip(parts[::2], parts[1::2])]
                out.append(cnt + parts[0].astype(F32))
            return tuple(out)

        cnts = _for_chunks(nc, body, tuple(jnp.zeros((rows, tq), F32) for _ in cands))
        return [cnt.sum(axis=0, keepdims=True) for cnt in cnts]

    def key_bit(i, carry, ref, to_cand):
        key, n_key = carry
        trial = key + lax.shift_left(jnp.int32(1), 31 - i)
        (n_trial,) = count_ge(ref, [to_cand(trial)])
        keep = n_trial >= topk
        return jnp.where(keep, trial, key), jnp.where(keep, n_trial, n_key)

    def hi_cand(trial):
        pattern = trial ^ ((trial >> 31) & 0x7FFF0000)
        return lax.bitcast_convert_type(pattern, F32).astype(BF16)

    def count(pred):
        def body(c, cnt):
            off, kpos = chunk_pos(c)
            hit = jnp.where(pred(sc_ref[pl.ds(off, ch), :], kpos), 1.0, 0.0)
            return cnt + hit.reshape(ch // SUBLANES, SUBLANES, tq).sum(axis=0)
        cnt = _for_chunks(nc, body, jnp.zeros((SUBLANES, tq), F32))
        return cnt.sum(axis=0, keepdims=True)

    def search(carry):
        carry = lax.fori_loop(0, 16, lambda i, cr: key_bit(i, cr, sh_ref, hi_cand), carry)
        return lax.fori_loop(16, 32, lambda i, cr: key_bit(i, cr, sc_ref, _key_to_float), carry)

    init = (jnp.full((1, tq), INT_MIN, I32), jnp.zeros((1, tq), F32))
    key, n_key = lax.cond(nc * tq <= topk, lambda cr: cr, search, init)
    tau = _key_to_float(jnp.maximum(key, KEY_MIN_FINITE))

    any_excess = jnp.max(jnp.where(n_key > topk, 1, 0)) > 0

    @pl.when(any_excess)
    def _():
        want = topk - count(lambda s, _: s > tau)

        def pos_bit(i, jmax):
            trial = jmax + lax.shift_left(jnp.int32(1), jbits - 1 - i)
            n_before = count(lambda s, kpos: jnp.where(s == tau, kpos, trial) < trial)
            return jnp.where(n_before <= want, trial, jmax)

        jmax = lax.fori_loop(0, jbits, pos_bit, jnp.zeros((1, tq), I32))

        def drop(c, carry):
            off, kpos = chunk_pos(c)
            s = sc_ref[pl.ds(off, ch), :]
            late_tie = jnp.where(s == tau, kpos, -1) >= jmax
            sc_ref[pl.ds(off, ch), :] = jnp.where(late_tie, -jnp.inf, s)
            return carry

        lax.fori_loop(0, nc, drop, 0)

    q = q_ref[0]
    group_of_lane = lax.broadcasted_iota(I32, (tq, LANES), 1) // HEAD_DIM
    qe = []
    for h in range(N_ATT_HEADS):
        g = h // HEADS_PER_GROUP
        t = q[:, (h // 2) * LANES:(h // 2 + 1) * LANES].astype(F32)
        if h % 2 != g:
            t = pltpu.roll(t, HEAD_DIM, 1)
        qe.append(jnp.where(group_of_lane == g, t, 0.0).astype(BF16))

    acc_ref[...] = jnp.zeros(acc_ref.shape, F32)

    def attend(c, carry, nchunks=1):
        ms, ls = carry
        off = pl.multiple_of(c * ch, ch)
        rows = nchunks * ch
        kc = ks_ref[0, pl.ds(off, rows), :]
        vt = vt_ref[0, c]
        if nchunks == 2:
            vt = jnp.concatenate([vt, vt_ref[0, c + 1]], axis=1)
        bias = jnp.where(sc_ref[pl.ds(off, rows), :] >= tau, 0.0, NEG_BIG)
        new_ms, new_ls = [], []
        lgs = [_dot_nt(kc, qe[h]) for h in range(N_ATT_HEADS)]
        for h in range(N_ATT_HEADS):
            lg = bias + lgs[h]
            m_new = jnp.maximum(ms[h], _col_reduce(lg, jnp.max))
            alpha = jnp.exp2(ms[h] - m_new)
            p = jnp.exp2(lg - m_new)
            new_ms.append(m_new)
            new_ls.append(alpha * ls[h] + _col_reduce(p, jnp.sum))
            cols = slice(h * tq, (h + 1) * tq)
            acc_ref[:, cols] = alpha * acc_ref[:, cols] + _dot(vt, p.astype(BF16))
        return tuple(new_ms), tuple(new_ls)

    init = (tuple(jnp.full((1, tq), NEG_BIG, F32) for _ in range(N_ATT_HEADS)),
            tuple(jnp.zeros((1, tq), F32) for _ in range(N_ATT_HEADS)))
    _, ls = _for_chunks(nc, attend, init, slab=True)

    lane = lax.broadcasted_iota(I32, (tq, LANES), 1)
    tiles = []
    for j in range(N_ATT_HEADS // 2):
        parts = []
        for p_ in range(2):
            h = 2 * j + p_
            blk = acc_ref[:, h * tq:(h + 1) * tq] * (1.0 / ls[h])
            if h // HEADS_PER_GROUP != p_:
                blk = jnp.concatenate([blk[HEAD_DIM:], blk[:HEAD_DIM]], axis=0)
            parts.append(blk.T)
        tiles.append(jnp.where(lane < HEAD_DIM, parts[0], parts[1]))
    o_ref[0] = jnp.concatenate(tiles, axis=1).astype(BF16)


def _dsa(qi, wi, q, ki, ks, vt):
    b, s, _ = q.shape
    topk = min(TOPK_MAX, s // 4)
    blk = lambda w: pl.BlockSpec((1, Q_TILE, w), lambda bi, qb: (bi, qb, 0))
    full = lambda w: pl.BlockSpec((1, s, w), lambda bi, qb: (bi, 0, 0))
    return pl.pallas_call(
        functools.partial(_dsa_kernel, topk=topk, jbits=int(s).bit_length()),
        grid=(b, s // Q_TILE),
        in_specs=[blk(IDX_WIDTH), blk(LANES), blk(ATT_WIDTH), full(IDX_WIDTH), full(KV_WIDTH),
                  pl.BlockSpec((1, s // Q_TILE, KV_WIDTH, Q_TILE), lambda bi, qb: (bi, 0, 0, 0))],
        out_specs=blk(ATT_WIDTH),
        out_shape=jax.ShapeDtypeStruct((b, s, ATT_WIDTH), BF16),
        scratch_shapes=[pltpu.VMEM((s, Q_TILE), F32), pltpu.VMEM((s, Q_TILE), BF16),
                        pltpu.VMEM((KV_WIDTH, N_ATT_HEADS * Q_TILE), F32)],
        compiler_params=pltpu.CompilerParams(
            dimension_semantics=("arbitrary", "arbitrary"), vmem_limit_bytes=VMEM_LIMIT),
        name="dsa",
    )(qi, wi, q, ki, ks, vt)


def _causal_conv3(u, w, carry_ref):
    prev = carry_ref[...]
    row = lax.broadcasted_iota(I32, prev.shape, 0)
    s1 = pltpu.roll(u, 1, 0)
    s2 = pltpu.roll(u, 2, 0)
    p1 = pltpu.roll(prev, 1, 0)
    p2 = pltpu.roll(prev, 2, 0)
    s1 = jnp.concatenate([jnp.where(row < 1, p1, s1[:SUBLANES]), s1[SUBLANES:]], axis=0)
    s2 = jnp.concatenate([jnp.where(row < 2, p2, s2[:SUBLANES]), s2[SUBLANES:]], axis=0)
    carry_ref[...] = u[u.shape[0] - SUBLANES:]
    return s2 * w[0:1] + s1 * w[1:2] + u * w[2:3]


def _merge_kernel(x_ref, g_ref, att_ref, conv_ref, qc_ref, km_ref, vm_ref, wg_ref, bg_ref, cw_ref,
                  wa_ref, wc_ref, wm_ref, wo_ref, o_ref, carry_ref):
    @pl.when(pl.program_id(1) == 0)
    def _():
        carry_ref[...] = jnp.zeros(carry_ref.shape, F32)

    x = x_ref[0]
    h = _rms(x, g_ref[...]).astype(BF16)

    y_att = _dot(att_ref[0], wa_ref[...])

    cin = conv_ref[0].astype(F32)
    c = CONV_CHANNELS
    cu = cin[:, c:2 * c] * cin[:, 2 * c:]
    yc = cin[:, :c] * _causal_conv3(cu, cw_ref[...], carry_ref)
    y_conv = _dot(yc.astype(BF16), wc_ref[...])

    qc, km, vm = qc_ref[0], km_ref[0], vm_ref[0]
    head_of_lane = lax.broadcasted_iota(I32, km.shape, 1) // HEAD_DIM
    zero_bf = jnp.zeros((), BF16)
    mem_out = jnp.zeros((x.shape[0], CROSS_WIDTH), F32)
    for hh in range(N_CROSS_HEADS):
        lg = _dot_nt(qc, jnp.where(head_of_lane == hh, km, zero_bf)) * (HEAD_DIM ** -0.5)
        e = jnp.exp(lg - lg.max(axis=-1, keepdims=True))
        p = e / e.sum(axis=-1, keepdims=True)
        mem_out = mem_out + _dot(p.astype(BF16), jnp.where(head_of_lane == hh, vm, zero_bf))
    y_mem = _dot(mem_out.astype(BF16), wm_ref[...])

    d = x.shape[1]
    merged = jnp.zeros(x.shape, F32)
    for i, y in enumerate((y_att, y_conv, y_mem)):
        pre = _dot(h, wg_ref[:, i * d:(i + 1) * d]) + bg_ref[:, i * d:(i + 1) * d]
        merged = merged + jax.nn.sigmoid(pre) * y
    o_ref[0] = x + _dot(merged.astype(BF16), wo_ref[...])


def _merge(x, g_mix, att, conv_in, qc, km, vm, wg, b_gate, conv_w, wa, wc, wm, wo, tm):
    b, s, d = x.shape
    m = km.shape[1]
    tok = lambda w: pl.BlockSpec((1, tm, w), lambda bi, si: (bi, si, 0))
    const = lambda shape: pl.BlockSpec(shape, lambda bi, si: (0,) * len(shape),
                                       pipeline_mode=pl.Buffered(1))
    memspec = pl.BlockSpec((1, m, CROSS_WIDTH), lambda bi, si: (bi, 0, 0))
    return pl.pallas_call(
        _merge_kernel,
        grid=(b, s // tm),
        in_specs=[tok(d), const((1, d)), tok(ATT_WIDTH), tok(3 * CONV_CHANNELS), tok(CROSS_WIDTH),
                  memspec, memspec, const((d, N_BRANCHES * d)), const((1, N_BRANCHES * d)),
                  const((CONV_K, CONV_CHANNELS)), const((ATT_WIDTH, d)), const((CONV_CHANNELS, d)),
                  const((CROSS_WIDTH, d)), const((d, d))],
        out_specs=tok(d),
        out_shape=jax.ShapeDtypeStruct((b, s, d), F32),
        scratch_shapes=[pltpu.VMEM((SUBLANES, CONV_CHANNELS), F32)],
        compiler_params=pltpu.CompilerParams(
            dimension_semantics=("arbitrary", "arbitrary"), vmem_limit_bytes=VMEM_LIMIT),
        name="merge",
    )(x, g_mix, att, conv_in, qc, km, vm, wg, b_gate, conv_w, wa, wc, wm, wo)


def _ffn_kernel(x_ref, g_ref, wu_ref, cw_ref, wd_ref, gf_ref, o_ref, carry_ref, act_ref, *, fc):
    @pl.when(pl.program_id(1) == 0)
    def _():
        carry_ref[...] = jnp.zeros(carry_ref.shape, F32)

    x = x_ref[0]
    h = _rms(x, g_ref[...]).astype(BF16)
    dff = wd_ref.shape[0]
    for c0 in range(0, dff, fc):
        halves = []
        for base in (0, dff):
            cols = slice(base + c0, base + c0 + fc)
            up = _dot(h, wu_ref[:, cols])
            halves.append(_causal_conv3(up, cw_ref[:, cols], carry_ref.at[:, cols]))
        gate, val = halves
        act_ref[:, c0:c0 + fc] = (gate * jax.nn.sigmoid(gate) * val).astype(BF16)
    y = x + _dot(act_ref[...], wd_ref[...])
    o_ref[0] = _rms(y, gf_ref[...])


def _ffn(x, g_ffn, wu, conv_w, wd, g_final, tm, fc):
    b, s, d = x.shape
    dff = wd.shape[0]
    tok = pl.BlockSpec((1, tm, d), lambda bi, si: (bi, si, 0))
    const = lambda shape: pl.BlockSpec(shape, lambda bi, si: (0,) * len(shape),
                                       pipeline_mode=pl.Buffered(1))
    return pl.pallas_call(
        functools.partial(_ffn_kernel, fc=fc),
        grid=(b, s // tm),
        in_specs=[tok, const((1, d)), const((d, 2 * dff)), const((CONV_K, 2 * dff)),
                  const((dff, d)), const((1, d))],
        out_specs=tok,
        out_shape=jax.ShapeDtypeStruct((b, s, d), F32),
        scratch_shapes=[pltpu.VMEM((SUBLANES, 2 * dff), F32), pltpu.VMEM((tm, dff), BF16)],
        compiler_params=pltpu.CompilerParams(
            dimension_semantics=("arbitrary", "arbitrary"), vmem_limit_bytes=VMEM_LIMIT),
        name="ffn",
    )(x, g_ffn, wu, conv_w, wd, g_final)


def _pack_in_proj_weight(w):
    sizes = (ATT_WIDTH, KV_WIDTH, KV_WIDTH, IDX_WIDTH, IDX_DIM, N_IDX_HEADS,
             3 * CONV_CHANNELS, CROSS_WIDTH)
    offs = np.concatenate([[0], np.cumsum(sizes)])
    wq, wk, wv, wqi, wki, wwi, wconv, wqc = [w[:, offs[i]:offs[i + 1]] for i in range(len(sizes))]
    idx_scale = (N_IDX_HEADS * IDX_DIM) ** -0.5
    att_scale = HEAD_DIM ** -0.5
    wwi = jnp.pad(wwi * idx_scale, ((0, 0), (0, LANES - N_IDX_HEADS)))
    w1 = jnp.concatenate([wq * att_scale, wwi, wk, wv, wqi, jnp.tile(wki, (1, N_IDX_HEADS)),
                          wconv, wqc], axis=1)
    return w1.astype(BF16), w[:, offs[-1]:].astype(BF16)


def _layer(x, mem, g_mix, w_in, b_gate, conv_w_short, w_att_out, w_conv_out, w_mem_out, w_o,
           g_mem, w_mem_kv, g_ffn, w_up, conv_w_ffn, w_down, g_final):
    b, s, d = x.shape
    tm1 = min(512, s)
    tm4 = min(512, s)
    tm5 = min(512, s)
    w1, wg = _pack_in_proj_weight(w_in)
    q, ks, vt, qi, ki, conv_in, qc, wi = _in_proj(x, g_mix[None], w1, tm1, Q_TILE)
    km, vm = _mem_kv(mem, g_mem[None], w_mem_kv.astype(BF16))
    att = _dsa(qi, wi, q, ki, ks, vt)
    x1 = _merge(x, g_mix[None], att, conv_in, qc, km, vm, wg, b_gate[None], conv_w_short,
                w_att_out.astype(BF16), w_conv_out.astype(BF16), w_mem_out.astype(BF16),
                w_o.astype(BF16), tm4)
    dff = w_down.shape[0]
    fc = 256 if dff % 256 == 0 else dff
    return _ffn(x1, g_ffn[None], w_up.astype(BF16), conv_w_ffn, w_down.astype(BF16),
                g_final[None], tm5, fc)


def kernel(x, mem, g_mix, w_in, b_gate, conv_w_short, w_att_out, w_conv_out, w_mem_out, w_o,
           g_mem, w_mem_kv, g_ffn, w_up, conv_w_ffn, w_down, g_final):
    depth = w_in.shape[0]
    assert depth == 1, "the final RMSNorm is fused into the last layer's ffn kernel"
    return _layer(x, mem, g_mix[0], w_in[0], b_gate[0], conv_w_short[0], w_att_out[0],
                  w_conv_out[0], w_mem_out[0], w_o[0], g_mem[0], w_mem_kv[0], g_ffn[0],
                  w_up[0], conv_w_ffn[0], w_down[0], g_final)
```

```python
import functools

import numpy as np
import jax
import jax.numpy as jnp
from jax import lax
from jax.experimental import pallas as pl
from jax.experimental.pallas import tpu as pltpu

F32 = jnp.float32
BF16 = jnp.bfloat16
I32 = jnp.int32

HEAD_DIM = 64
N_ATT_HEADS = 6
N_KV_GROUPS = 2
HEADS_PER_GROUP = N_ATT_HEADS // N_KV_GROUPS
ATT_WIDTH = N_ATT_HEADS * HEAD_DIM
KV_WIDTH = N_KV_GROUPS * HEAD_DIM
N_IDX_HEADS = 8
IDX_DIM = 32
IDX_WIDTH = N_IDX_HEADS * IDX_DIM
TOPK_MAX = 256
CONV_CHANNELS = 384
CONV_K = 3
N_CROSS_HEADS = 4
CROSS_WIDTH = N_CROSS_HEADS * HEAD_DIM
N_BRANCHES = 3
ROPE_THETA = 500000.0
ROPE_FRACTION = 4
NORM_EPS = 1e-6

LANES = 128
SUBLANES = 8
VMEM_LIMIT = 56 * 1024 * 1024

Q_TILE = 256
DSA_BATCH = 2
NEG_BIG = -1e30
LOG2_E = 1.4426950408889634
INT_MIN = -2 ** 31
KEY_MIN_FINITE = -2139095040
HI16_MASK = -65536


def _rms(x, g):
    return x * lax.rsqrt(jnp.mean(x * x, axis=-1, keepdims=True) + NORM_EPS) * g


def _dot_nt(a, b):
    return lax.dot_general(a, b, (((1,), (1,)), ((), ())), preferred_element_type=F32)


def _dot(a, b):
    return jnp.dot(a, b, preferred_element_type=F32)


def _rope_tile(x, c, sa, sb, half):
    return x * c + pltpu.roll(x, LANES - half, 1) * sa + pltpu.roll(x, half, 1) * sb


def _rope(x, c, sa, sb, half):
    tiles = [_rope_tile(x[:, i:i + LANES], c, sa, sb, half) for i in range(0, x.shape[1], LANES)]
    return tiles[0] if len(tiles) == 1 else jnp.concatenate(tiles, axis=1)


_OFF_Q = 0
_OFF_WI = _OFF_Q + ATT_WIDTH
_OFF_KV = _OFF_WI + LANES
_OFF_QI = _OFF_KV + 2 * KV_WIDTH
_OFF_KI = _OFF_QI + IDX_WIDTH
_OFF_CONV = _OFF_KI + IDX_WIDTH
_OFF_QC = _OFF_CONV + 3 * CONV_CHANNELS
_W1_WIDTH = _OFF_QC + CROSS_WIDTH


def _in_proj_kernel(x_ref, g_ref, w_ref, ca_ref, saa_ref, sba_ref, ci_ref, sai_ref, sbi_ref,
                    q_ref, ks_ref, vt_ref, qi_ref, ki_ref, conv_ref, qc_ref, wi_ref, *, kchunk):
    h = _rms(x_ref[0], g_ref[...]).astype(BF16)
    ca, saa, sba = ca_ref[...], saa_ref[...], sba_ref[...]
    ci, sai, sbi = ci_ref[...], sai_ref[...], sbi_ref[...]
    half_a = HEAD_DIM // ROPE_FRACTION // 2
    half_i = IDX_DIM // ROPE_FRACTION // 2

    a = _dot(h, w_ref[:, _OFF_Q:_OFF_KV])
    q_ref[0] = (_rope(a[:, :ATT_WIDTH], ca, saa, sba, half_a) * LOG2_E).astype(BF16)
    wi_ref[0] = a[:, ATT_WIDTH:]

    kv = _dot(h, w_ref[:, _OFF_KV:_OFF_QI])
    ks_ref[0] = _rope(kv[:, :KV_WIDTH], ca, saa, sba, half_a).astype(BF16)
    v = kv[:, KV_WIDTH:]
    for i in range(v.shape[0] // kchunk):
        vt_ref[0, i] = v[i * kchunk:(i + 1) * kchunk].T.astype(BF16)

    idx = _dot(h, w_ref[:, _OFF_QI:_OFF_CONV])
    qi_ref[0] = _rope(idx[:, :IDX_WIDTH], ci, sai, sbi, half_i).astype(BF16)
    ki_ref[0] = _rope(idx[:, IDX_WIDTH:], ci, sai, sbi, half_i).astype(BF16)

    cq = _dot(h, w_ref[:, _OFF_CONV:_W1_WIDTH])
    conv_ref[0] = cq[:, :3 * CONV_CHANNELS].astype(BF16)
    qc_ref[0] = cq[:, 3 * CONV_CHANNELS:].astype(BF16)


def _rope_tables(s, period, rot_dim):
    half = rot_dim // 2
    inv_freq = ROPE_THETA ** (-jnp.arange(half, dtype=F32) / half)
    ang = jnp.arange(s).astype(F32)[:, None] * inv_freq[None, :]
    cos, sin = jnp.cos(ang), jnp.sin(ang)
    j = np.arange(LANES) % period
    first = jnp.asarray(j < half)[None, :]
    second = jnp.asarray((j >= half) & (j < rot_dim))[None, :]
    fidx = np.where(j < half, j, np.where(j < rot_dim, j - half, 0))
    cos_l, sin_l = cos[:, fidx], sin[:, fidx]
    c = jnp.where(first | second, cos_l, 1.0)
    sa = jnp.where(first, -sin_l, 0.0)
    sb = jnp.where(second, sin_l, 0.0)
    return c, sa, sb


def _in_proj(x, g_mix, w1, tm, kchunk):
    b, s, d = x.shape
    tabs = _rope_tables(s, HEAD_DIM, HEAD_DIM // ROPE_FRACTION) + \
        _rope_tables(s, IDX_DIM, IDX_DIM // ROPE_FRACTION)
    tok = lambda w: pl.BlockSpec((1, tm, w), lambda si, bi: (bi, si, 0))
    tab = pl.BlockSpec((tm, LANES), lambda si, bi: (si, 0))
    const = lambda shape: pl.BlockSpec(shape, lambda si, bi: (0,) * len(shape),
                                       pipeline_mode=pl.Buffered(1))
    out_shape = (
        jax.ShapeDtypeStruct((b, s, ATT_WIDTH), BF16),
        jax.ShapeDtypeStruct((b, s, KV_WIDTH), BF16),
        jax.ShapeDtypeStruct((b, s // kchunk, KV_WIDTH, kchunk), BF16),
        jax.ShapeDtypeStruct((b, s, IDX_WIDTH), BF16),
        jax.ShapeDtypeStruct((b, s, IDX_WIDTH), BF16),
        jax.ShapeDtypeStruct((b, s, 3 * CONV_CHANNELS), BF16),
        jax.ShapeDtypeStruct((b, s, CROSS_WIDTH), BF16),
        jax.ShapeDtypeStruct((b, s, LANES), F32),
    )
    out_specs = (tok(ATT_WIDTH), tok(KV_WIDTH),
                 pl.BlockSpec((1, tm // kchunk, KV_WIDTH, kchunk), lambda si, bi: (bi, si, 0, 0)),
                 tok(IDX_WIDTH), tok(IDX_WIDTH), tok(3 * CONV_CHANNELS), tok(CROSS_WIDTH), tok(LANES))
    return pl.pallas_call(
        functools.partial(_in_proj_kernel, kchunk=kchunk),
        grid=(s // tm, b),
        in_specs=[tok(d), const((1, d)), const((d, _W1_WIDTH))] + [tab] * 6,
        out_specs=out_specs,
        out_shape=out_shape,
        compiler_params=pltpu.CompilerParams(
            dimension_semantics=("arbitrary", "arbitrary"), vmem_limit_bytes=VMEM_LIMIT),
        name="in_proj",
    )(x, g_mix, w1, *tabs)


def _mem_kv_kernel(mem_ref, g_ref, w_ref, km_ref, vm_ref):
    h = _rms(mem_ref[0], g_ref[...]).astype(BF16)
    kv = _dot(h, w_ref[...])
    km_ref[0] = kv[:, :CROSS_WIDTH].astype(BF16)
    vm_ref[0] = kv[:, CROSS_WIDTH:].astype(BF16)


def _mem_kv(mem, g_mem, w_kv):
    b, m, d = mem.shape
    const = lambda shape: pl.BlockSpec(shape, lambda bi: (0,) * len(shape))
    spec = pl.BlockSpec((1, m, CROSS_WIDTH), lambda bi: (bi, 0, 0))
    return pl.pallas_call(
        _mem_kv_kernel,
        grid=(b,),
        in_specs=[pl.BlockSpec((1, m, d), lambda bi: (bi, 0, 0)), const((1, d)),
                  const((d, 2 * CROSS_WIDTH))],
        out_specs=(spec, spec),
        out_shape=(jax.ShapeDtypeStruct((b, m, CROSS_WIDTH), BF16),) * 2,
        compiler_params=pltpu.CompilerParams(dimension_semantics=("arbitrary",)),
        name="mem_kv",
    )(mem, g_mem, w_kv)


def _key_to_float(key):
    return lax.bitcast_convert_type(key ^ ((key >> 31) & 0x7FFFFFFF), F32)


def _col_reduce(x, op):
    rows, n = x.shape
    r = op(x.reshape(SUBLANES, rows // (SUBLANES * SUBLANES), SUBLANES, n), axis=1)
    return op(op(r, axis=0), axis=0, keepdims=True)


def _for_chunks(n, body, init, slab=False):
    def pair(i, carry):
        if slab:
            return body(2 * i, carry, 2)
        return body(2 * i + 1, body(2 * i, carry))
    carry = lax.fori_loop(0, n // 2, pair, init)
    return lax.cond(n % 2 == 1, lambda cr: body(n - 1, cr), lambda cr: cr, carry)


def _dsa_kernel(qi_ref, wi_ref, q_ref, ki_ref, ks_ref, vt_ref, o_ref, sc_ref, sh_ref, acc_ref,
                *, topk, jbits):
    tq = ch = Q_TILE
    batch = range(qi_ref.shape[0])
    qb = pl.program_id(1)
    nc = qb + 1
    qpos = qb * tq + lax.broadcasted_iota(I32, (1, tq), 1)

    def chunk_pos(c):
        off = pl.multiple_of(c * ch, ch)
        return off, off + lax.broadcasted_iota(I32, (ch, tq), 0)

    head_of_lane = lax.broadcasted_iota(I32, (tq, IDX_WIDTH), 1) // IDX_DIM
    zero_bf = jnp.zeros((), BF16)
    qi_heads = [[jnp.where(head_of_lane == h, qi_ref[bb], zero_bf) for h in range(N_IDX_HEADS)]
                for bb in batch]
    wit = [wi_ref[bb].T for bb in batch]
    wrow = [[wit[bb][h:h + 1, :] for h in range(N_IDX_HEADS)] for bb in batch]

    def scores(bb, c, nchunks=1):
        off = pl.multiple_of(c * ch, ch)
        kc = ki_ref[bb, pl.ds(off, nchunks * ch), :]
        acc = jnp.maximum(_dot_nt(kc, qi_heads[bb][0]), 0.0) * wrow[bb][0]
        for h in range(1, N_IDX_HEADS):
            acc = acc + jnp.maximum(_dot_nt(kc, qi_heads[bb][h]), 0.0) * wrow[bb][h]
        return off, acc

    def put_scores(bb, off, s):
        sc_ref[bb, pl.ds(off, s.shape[0]), :] = s
        hi = lax.bitcast_convert_type(lax.bitcast_convert_type(s, I32) & HI16_MASK, F32)
        sh_ref[bb, pl.ds(off, s.shape[0]), :] = hi.astype(BF16)

    def score_chunks(c, carry, nchunks=1):
        for bb in batch:
            put_scores(bb, *scores(bb, c, nchunks))
        return carry

    _for_chunks(qb, score_chunks, 0, slab=True)
    for bb in batch:
        off, acc = scores(bb, qb)
        put_scores(bb, off, jnp.where(chunk_pos(qb)[1] <= qpos, acc, -jnp.inf))

    def count_ge(ref, cands):
        rows = SUBLANES * (4 // ref.dtype.itemsize)
        one, zero = jnp.ones((), ref.dtype), jnp.zeros((), ref.dtype)

        def body(c, cnts):
            off, _ = chunk_pos(c)
            out = []
            for bb in batch:
                hit = jnp.where(ref[bb, pl.ds(off, ch), :] >= cands[bb], one, zero)
                parts = [hit[r:r + rows] for r in range(0, ch, rows)]
                while len(parts) > 1:
                    parts = [a + b for a, b in zip(parts[::2], parts[1::2])]
                out.append(cnts[bb] + parts[0].astype(F32))
            return tuple(out)

        cnts = _for_chunks(nc, body, tuple(jnp.zeros((rows, tq), F32) for _ in batch))
        return [cnt.sum(axis=0, keepdims=True) for cnt in cnts]

    def key_bit(i, carry, ref, to_cand):
        keys, n_keys = carry
        bit = lax.shift_left(jnp.int32(1), 31 - i)
        trials = [key + bit for key in keys]
        n_trials = count_ge(ref, [to_cand(trial) for trial in trials])
        keeps = [n_trial >= topk for n_trial in n_trials]
        return (tuple(jnp.where(keeps[bb], trials[bb], keys[bb]) for bb in batch),
                tuple(jnp.where(keeps[bb], n_trials[bb], n_keys[bb]) for bb in batch))

    def hi_cand(trial):
        pattern = trial ^ ((trial >> 31) & 0x7FFF0000)
        return lax.bitcast_convert_type(pattern, F32).astype(BF16)

    def count(pred):
        def body(c, cnts):
            off, kpos = chunk_pos(c)
            hits = [jnp.where(pred(bb, sc_ref[bb, pl.ds(off, ch), :], kpos), 1.0, 0.0) for bb in batch]
            return tuple(cnts[bb] + hits[bb].reshape(ch // SUBLANES, SUBLANES, tq).sum(axis=0)
                         for bb in batch)
        cnts = _for_chunks(nc, body, tuple(jnp.zeros((SUBLANES, tq), F32) for _ in batch))
        return [cnt.sum(axis=0, keepdims=True) for cnt in cnts]

    def search(carry):
        carry = lax.fori_loop(0, 16, lambda i, cr: key_bit(i, cr, sh_ref, hi_cand), carry)
        return lax.fori_loop(16, 32, lambda i, cr: key_bit(i, cr, sc_ref, _key_to_float), carry)

    init = (tuple(jnp.full((1, tq), INT_MIN, I32) for _ in batch),
            tuple(jnp.zeros((1, tq), F32) for _ in batch))
    keys, n_keys = lax.cond(nc * tq <= topk, lambda cr: cr, search, init)
    taus = [_key_to_float(jnp.maximum(key, KEY_MIN_FINITE)) for key in keys]

    excess = sum(jnp.where(n_key > topk, 1, 0) for n_key in n_keys)

    @pl.when(jnp.max(excess) > 0)
    def _():
        n_above = count(lambda bb, s, _: s > taus[bb])
        wants = [topk - n for n in n_above]

        def pos_bit(i, jmaxs):
            bit = lax.shift_left(jnp.int32(1), jbits - 1 - i)
            trials = [jmax + bit for jmax in jmaxs]
            n_before = count(lambda bb, s, kpos: jnp.where(s == taus[bb], kpos, trials[bb]) < trials[bb])
            return tuple(jnp.where(n_before[bb] <= wants[bb], trials[bb], jmaxs[bb]) for bb in batch)

        jmaxs = lax.fori_loop(0, jbits, pos_bit, tuple(jnp.zeros((1, tq), I32) for _ in batch))

        def drop(c, carry):
            off, kpos = chunk_pos(c)
            for bb in batch:
                s = sc_ref[bb, pl.ds(off, ch), :]
                late_tie = jnp.where(s == taus[bb], kpos, -1) >= jmaxs[bb]
                sc_ref[bb, pl.ds(off, ch), :] = jnp.where(late_tie, -jnp.inf, s)
            return carry

        lax.fori_loop(0, nc, drop, 0)

    group_of_lane = lax.broadcasted_iota(I32, (tq, LANES), 1) // HEAD_DIM
    qe = []
    for bb in batch:
        q = q_ref[bb]
        qe.append([])
        for h in range(N_ATT_HEADS):
            g = h // HEADS_PER_GROUP
            t = q[:, (h // 2) * LANES:(h // 2 + 1) * LANES].astype(F32)
            if h % 2 != g:
                t = pltpu.roll(t, HEAD_DIM, 1)
            qe[bb].append(jnp.where(group_of_lane == g, t, 0.0).astype(BF16))

    acc_ref[...] = jnp.zeros(acc_ref.shape, F32)
    streams = [(bb, h) for bb in batch for h in range(N_ATT_HEADS)]

    def attend(c, carry, nchunks=1):
        ms, ls = carry
        off = pl.multiple_of(c * ch, ch)
        rows = nchunks * ch
        kcs = [ks_ref[bb, pl.ds(off, rows), :] for bb in batch]
        lgs = [_dot_nt(kcs[bb], qe[bb][h]) for bb, h in streams]
        vts, biases = [], []
        for bb in batch:
            vt = vt_ref[bb, c]
            if nchunks == 2:
                vt = jnp.concatenate([vt, vt_ref[bb, c + 1]], axis=1)
            vts.append(vt)
            biases.append(jnp.where(sc_ref[bb, pl.ds(off, rows), :] >= taus[bb], 0.0, NEG_BIG))
        new_ms, new_ls = [], []
        for i, (bb, h) in enumerate(streams):
            lg = biases[bb] + lgs[i]
            m_new = jnp.maximum(ms[i], _col_reduce(lg, jnp.max))
            alpha = jnp.exp2(ms[i] - m_new)
            p = jnp.exp2(lg - m_new)
            new_ms.append(m_new)
            new_ls.append(alpha * ls[i] + _col_reduce(p, jnp.sum))
            cols = slice(h * tq, (h + 1) * tq)
            acc_ref[bb, :, cols] = alpha * acc_ref[bb, :, cols] + _dot(vts[bb], p.astype(BF16))
        return tuple(new_ms), tuple(new_ls)

    init = (tuple(jnp.full((1, tq), NEG_BIG, F32) for _ in streams),
            tuple(jnp.zeros((1, tq), F32) for _ in streams))
    _, ls = _for_chunks(nc, attend, init, slab=True)

    lane = lax.broadcasted_iota(I32, (tq, LANES), 1)
    for bb in batch:
        tiles = []
        for j in range(N_ATT_HEADS // 2):
            parts = []
            for p_ in range(2):
                h = 2 * j + p_
                blk = acc_ref[bb, :, h * tq:(h + 1) * tq] * (1.0 / ls[streams.index((bb, h))])
                if h // HEADS_PER_GROUP != p_:
                    blk = jnp.concatenate([blk[HEAD_DIM:], blk[:HEAD_DIM]], axis=0)
                parts.append(blk.T)
            tiles.append(jnp.where(lane < HEAD_DIM, parts[0], parts[1]))
        o_ref[bb] = jnp.concatenate(tiles, axis=1).astype(BF16)


def _dsa(qi, wi, q, ki, ks, vt):
    b, s, _ = q.shape
    topk = min(TOPK_MAX, s // 4)
    nb = DSA_BATCH if b % DSA_BATCH == 0 else 1
    blk = lambda w: pl.BlockSpec((nb, Q_TILE, w), lambda bi, qb: (bi, qb, 0))
    full = lambda w: pl.BlockSpec((nb, s, w), lambda bi, qb: (bi, 0, 0))
    return pl.pallas_call(
        functools.partial(_dsa_kernel, topk=topk, jbits=int(s).bit_length()),
        grid=(b // nb, s // Q_TILE),
        in_specs=[blk(IDX_WIDTH), blk(LANES), blk(ATT_WIDTH), full(IDX_WIDTH), full(KV_WIDTH),
                  pl.BlockSpec((nb, s // Q_TILE, KV_WIDTH, Q_TILE), lambda bi, qb: (bi, 0, 0, 0))],
        out_specs=blk(ATT_WIDTH),
        out_shape=jax.ShapeDtypeStruct((b, s, ATT_WIDTH), BF16),
        scratch_shapes=[pltpu.VMEM((nb, s, Q_TILE), F32), pltpu.VMEM((nb, s, Q_TILE), BF16),
                        pltpu.VMEM((nb, KV_WIDTH, N_ATT_HEADS * Q_TILE), F32)],
        compiler_params=pltpu.CompilerParams(
            dimension_semantics=("arbitrary", "arbitrary"), vmem_limit_bytes=VMEM_LIMIT),
        name="dsa",
    )(qi, wi, q, ki, ks, vt)


def _causal_conv3(u, w, carry_ref):
    prev = carry_ref[...]
    row = lax.broadcasted_iota(I32, prev.shape, 0)
    s1 = pltpu.roll(u, 1, 0)
    s2 = pltpu.roll(u, 2, 0)
    p1 = pltpu.roll(prev, 1, 0)
    p2 = pltpu.roll(prev, 2, 0)
    s1 = jnp.concatenate([jnp.where(row < 1, p1, s1[:SUBLANES]), s1[SUBLANES:]], axis=0)
    s2 = jnp.concatenate([jnp.where(row < 2, p2, s2[:SUBLANES]), s2[SUBLANES:]], axis=0)
    carry_ref[...] = u[u.shape[0] - SUBLANES:]
    return s2 * w[0:1] + s1 * w[1:2] + u * w[2:3]


def _merge_kernel(x_ref, g_ref, att_ref, conv_ref, qc_ref, km_ref, vm_ref, wg_ref, bg_ref, cw_ref,
                  wa_ref, wc_ref, wm_ref, wo_ref, o_ref, carry_ref):
    @pl.when(pl.program_id(1) == 0)
    def _():
        carry_ref[...] = jnp.zeros(carry_ref.shape, F32)

    x = x_ref[0]
    h = _rms(x, g_ref[...]).astype(BF16)

    y_att = _dot(att_ref[0], wa_ref[...])

    cin = conv_ref[0].astype(F32)
    c = CONV_CHANNELS
    cu = cin[:, c:2 * c] * cin[:, 2 * c:]
    yc = cin[:, :c] * _causal_conv3(cu, cw_ref[...], carry_ref)
    y_conv = _dot(yc.astype(BF16), wc_ref[...])

    qc, km, vm = qc_ref[0], km_ref[0], vm_ref[0]
    head_of_lane = lax.broadcasted_iota(I32, km.shape, 1) // HEAD_DIM
    zero_bf = jnp.zeros((), BF16)
    mem_out = jnp.zeros((x.shape[0], CROSS_WIDTH), F32)
    for hh in range(N_CROSS_HEADS):
        lg = _dot_nt(qc, jnp.where(head_of_lane == hh, km, zero_bf)) * (HEAD_DIM ** -0.5)
        e = jnp.exp(lg - lg.max(axis=-1, keepdims=True))
        p = e / e.sum(axis=-1, keepdims=True)
        mem_out = mem_out + _dot(p.astype(BF16), jnp.where(head_of_lane == hh, vm, zero_bf))
    y_mem = _dot(mem_out.astype(BF16), wm_ref[...])

    d = x.shape[1]
    merged = jnp.zeros(x.shape, F32)
    for i, y in enumerate((y_att, y_conv, y_mem)):
        pre = _dot(h, wg_ref[:, i * d:(i + 1) * d]) + bg_ref[:, i * d:(i + 1) * d]
        merged = merged + jax.nn.sigmoid(pre) * y
    o_ref[0] = x + _dot(merged.astype(BF16), wo_ref[...])


def _merge(x, g_mix, att, conv_in, qc, km, vm, wg, b_gate, conv_w, wa, wc, wm, wo, tm):
    b, s, d = x.shape
    m = km.shape[1]
    tok = lambda w: pl.BlockSpec((1, tm, w), lambda bi, si: (bi, si, 0))
    const = lambda shape: pl.BlockSpec(shape, lambda bi, si: (0,) * len(shape),
                                       pipeline_mode=pl.Buffered(1))
    memspec = pl.BlockSpec((1, m, CROSS_WIDTH), lambda bi, si: (bi, 0, 0))
    return pl.pallas_call(
        _merge_kernel,
        grid=(b, s // tm),
        in_specs=[tok(d), const((1, d)), tok(ATT_WIDTH), tok(3 * CONV_CHANNELS), tok(CROSS_WIDTH),
                  memspec, memspec, const((d, N_BRANCHES * d)), const((1, N_BRANCHES * d)),
                  const((CONV_K, CONV_CHANNELS)), const((ATT_WIDTH, d)), const((CONV_CHANNELS, d)),
                  const((CROSS_WIDTH, d)), const((d, d))],
        out_specs=tok(d),
        out_shape=jax.ShapeDtypeStruct((b, s, d), F32),
        scratch_shapes=[pltpu.VMEM((SUBLANES, CONV_CHANNELS), F32)],
        compiler_params=pltpu.CompilerParams(
            dimension_semantics=("arbitrary", "arbitrary"), vmem_limit_bytes=VMEM_LIMIT),
        name="merge",
    )(x, g_mix, att, conv_in, qc, km, vm, wg, b_gate, conv_w, wa, wc, wm, wo)


def _ffn_kernel(x_ref, g_ref, wu_ref, cw_ref, wd_ref, gf_ref, o_ref, carry_ref, act_ref, *, fc):
    @pl.when(pl.program_id(1) == 0)
    def _():
        carry_ref[...] = jnp.zeros(carry_ref.shape, F32)

    x = x_ref[0]
    h = _rms(x, g_ref[...]).astype(BF16)
    dff = wd_ref.shape[0]
    for c0 in range(0, dff, fc):
        halves = []
        for base in (0, dff):
            cols = slice(base + c0, base + c0 + fc)
            up = _dot(h, wu_ref[:, cols])
            halves.append(_causal_conv3(up, cw_ref[:, cols], carry_ref.at[:, cols]))
        gate, val = halves
        act_ref[:, c0:c0 + fc] = (gate * jax.nn.sigmoid(gate) * val).astype(BF16)
    y = x + _dot(act_ref[...], wd_ref[...])
    o_ref[0] = _rms(y, gf_ref[...])


def _ffn(x, g_ffn, wu, conv_w, wd, g_final, tm, fc):
    b, s, d = x.shape
    dff = wd.shape[0]
    tok = pl.BlockSpec((1, tm, d), lambda bi, si: (bi, si, 0))
    const = lambda shape: pl.BlockSpec(shape, lambda bi, si: (0,) * len(shape),
                                       pipeline_mode=pl.Buffered(1))
    return pl.pallas_call(
        functools.partial(_ffn_kernel, fc=fc),
        grid=(b, s // tm),
        in_specs=[tok, const((1, d)), const((d, 2 * dff)), const((CONV_K, 2 * dff)),
                  const((dff, d)), const((1, d))],
        out_specs=tok,
        out_shape=jax.ShapeDtypeStruct((b, s, d), F32),
        scratch_shapes=[pltpu.VMEM((SUBLANES, 2 * dff), F32), pltpu.VMEM((tm, dff), BF16)],
        compiler_params=pltpu.CompilerParams(
            dimension_semantics=("arbitrary", "arbitrary"), vmem_limit_bytes=VMEM_LIMIT),
        name="ffn",
    )(x, g_ffn, wu, conv_w, wd, g_final)


def _pack_in_proj_weight(w):
    sizes = (ATT_WIDTH, KV_WIDTH, KV_WIDTH, IDX_WIDTH, IDX_DIM, N_IDX_HEADS,
             3 * CONV_CHANNELS, CROSS_WIDTH)
    offs = np.concatenate([[0], np.cumsum(sizes)])
    wq, wk, wv, wqi, wki, wwi, wconv, wqc = [w[:, offs[i]:offs[i + 1]] for i in range(len(sizes))]
    idx_scale = (N_IDX_HEADS * IDX_DIM) ** -0.5
    att_scale = HEAD_DIM ** -0.5
    wwi = jnp.pad(wwi * idx_scale, ((0, 0), (0, LANES - N_IDX_HEADS)))
    w1 = jnp.concatenate([wq * att_scale, wwi, wk, wv, wqi, jnp.tile(wki, (1, N_IDX_HEADS)),
                          wconv, wqc], axis=1)
    return w1.astype(BF16), w[:, offs[-1]:].astype(BF16)


def _layer(x, mem, g_mix, w_in, b_gate, conv_w_short, w_att_out, w_conv_out, w_mem_out, w_o,
           g_mem, w_mem_kv, g_ffn, w_up, conv_w_ffn, w_down, g_final):
    b, s, d = x.shape
    tm1 = min(512, s)
    tm4 = min(512, s)
    tm5 = min(512, s)
    w1, wg = _pack_in_proj_weight(w_in)
    q, ks, vt, qi, ki, conv_in, qc, wi = _in_proj(x, g_mix[None], w1, tm1, Q_TILE)
    km, vm = _mem_kv(mem, g_mem[None], w_mem_kv.astype(BF16))
    att = _dsa(qi, wi, q, ki, ks, vt)
    x1 = _merge(x, g_mix[None], att, conv_in, qc, km, vm, wg, b_gate[None], conv_w_short,
                w_att_out.astype(BF16), w_conv_out.astype(BF16), w_mem_out.astype(BF16),
                w_o.astype(BF16), tm4)
    dff = w_down.shape[0]
    fc = 256 if dff % 256 == 0 else dff
    return _ffn(x1, g_ffn[None], w_up.astype(BF16), conv_w_ffn, w_down.astype(BF16),
                g_final[None], tm5, fc)


def kernel(x, mem, g_mix, w_in, b_gate, conv_w_short, w_att_out, w_conv_out, w_mem_out, w_o,
           g_mem, w_mem_kv, g_ffn, w_up, conv_w_ffn, w_down, g_final):
    depth = w_in.shape[0]
    assert depth == 1, "the final RMSNorm is fused into the last layer's ffn kernel"
    return _layer(x, mem, g_mix[0], w_in[0], b_gate[0], conv_w_short[0], w_att_out[0],
                  w_conv_out[0], w_mem_out[0], w_o[0], g_mem[0], w_mem_kv[0], g_ffn[0],
                  w_up[0], conv_w_ffn[0], w_down[0], g_final)
```

```python
import functools

import numpy as np
import jax
import jax.numpy as jnp
from jax import lax
from jax.experimental import pallas as pl
from jax.experimental.pallas import tpu as pltpu

F32 = jnp.float32
BF16 = jnp.bfloat16
I32 = jnp.int32

HEAD_DIM = 64
N_ATT_HEADS = 6
N_KV_GROUPS = 2
HEADS_PER_GROUP = N_ATT_HEADS // N_KV_GROUPS
ATT_WIDTH = N_ATT_HEADS * HEAD_DIM
KV_WIDTH = N_KV_GROUPS * HEAD_DIM
N_IDX_HEADS = 8
IDX_DIM = 32
IDX_WIDTH = N_IDX_HEADS * IDX_DIM
TOPK_MAX = 256
CONV_CHANNELS = 384
CONV_K = 3
N_CROSS_HEADS = 4
CROSS_WIDTH = N_CROSS_HEADS * HEAD_DIM
N_BRANCHES = 3
ROPE_THETA = 500000.0
ROPE_FRACTION = 4
NORM_EPS = 1e-6

LANES = 128
SUBLANES = 8
VMEM_LIMIT = 56 * 1024 * 1024

Q_TILE = 256
DSA_BATCH = 4
NEG_BIG = -1e30
LOG2_E = 1.4426950408889634
INT_MIN = -2 ** 31
KEY_MIN_FINITE = -2139095040
HI16_MASK = -65536


def _rms(x, g):
    return x * lax.rsqrt(jnp.mean(x * x, axis=-1, keepdims=True) + NORM_EPS) * g


def _dot_nt(a, b):
    return lax.dot_general(a, b, (((1,), (1,)), ((), ())), preferred_element_type=F32)


def _dot(a, b):
    return jnp.dot(a, b, preferred_element_type=F32)


def _rope_tile(x, c, sa, sb, half):
    return x * c + pltpu.roll(x, LANES - half, 1) * sa + pltpu.roll(x, half, 1) * sb


def _rope(x, c, sa, sb, half):
    tiles = [_rope_tile(x[:, i:i + LANES], c, sa, sb, half) for i in range(0, x.shape[1], LANES)]
    return tiles[0] if len(tiles) == 1 else jnp.concatenate(tiles, axis=1)


_OFF_Q = 0
_OFF_WI = _OFF_Q + ATT_WIDTH
_OFF_KV = _OFF_WI + LANES
_OFF_QI = _OFF_KV + 2 * KV_WIDTH
_OFF_KI = _OFF_QI + IDX_WIDTH
_OFF_CONV = _OFF_KI + IDX_WIDTH
_OFF_QC = _OFF_CONV + 3 * CONV_CHANNELS
_W1_WIDTH = _OFF_QC + CROSS_WIDTH


def _in_proj_kernel(x_ref, g_ref, w_ref, ca_ref, saa_ref, sba_ref, ci_ref, sai_ref, sbi_ref,
                    q_ref, ks_ref, vt_ref, qi_ref, ki_ref, conv_ref, qc_ref, wi_ref, *, kchunk):
    h = _rms(x_ref[0], g_ref[...]).astype(BF16)
    ca, saa, sba = ca_ref[...], saa_ref[...], sba_ref[...]
    ci, sai, sbi = ci_ref[...], sai_ref[...], sbi_ref[...]
    half_a = HEAD_DIM // ROPE_FRACTION // 2
    half_i = IDX_DIM // ROPE_FRACTION // 2

    a = _dot(h, w_ref[:, _OFF_Q:_OFF_KV])
    q_ref[0] = (_rope(a[:, :ATT_WIDTH], ca, saa, sba, half_a) * LOG2_E).astype(BF16)
    wi_ref[0] = a[:, ATT_WIDTH:]

    kv = _dot(h, w_ref[:, _OFF_KV:_OFF_QI])
    ks_ref[0] = _rope(kv[:, :KV_WIDTH], ca, saa, sba, half_a).astype(BF16)
    v = kv[:, KV_WIDTH:]
    for i in range(v.shape[0] // kchunk):
        vt_ref[0, i] = v[i * kchunk:(i + 1) * kchunk].T.astype(BF16)

    idx = _dot(h, w_ref[:, _OFF_QI:_OFF_CONV])
    qi_ref[0] = _rope(idx[:, :IDX_WIDTH], ci, sai, sbi, half_i).astype(BF16)
    ki_ref[0] = _rope(idx[:, IDX_WIDTH:], ci, sai, sbi, half_i).astype(BF16)

    cq = _dot(h, w_ref[:, _OFF_CONV:_W1_WIDTH])
    conv_ref[0] = cq[:, :3 * CONV_CHANNELS].astype(BF16)
    qc_ref[0] = cq[:, 3 * CONV_CHANNELS:].astype(BF16)


def _rope_tables(s, period, rot_dim):
    half = rot_dim // 2
    inv_freq = ROPE_THETA ** (-jnp.arange(half, dtype=F32) / half)
    ang = jnp.arange(s).astype(F32)[:, None] * inv_freq[None, :]
    cos, sin = jnp.cos(ang), jnp.sin(ang)
    j = np.arange(LANES) % period
    first = jnp.asarray(j < half)[None, :]
    second = jnp.asarray((j >= half) & (j < rot_dim))[None, :]
    fidx = np.where(j < half, j, np.where(j < rot_dim, j - half, 0))
    cos_l, sin_l = cos[:, fidx], sin[:, fidx]
    c = jnp.where(first | second, cos_l, 1.0)
    sa = jnp.where(first, -sin_l, 0.0)
    sb = jnp.where(second, sin_l, 0.0)
    return c, sa, sb


def _in_proj(x, g_mix, w1, tm, kchunk):
    b, s, d = x.shape
    tabs = _rope_tables(s, HEAD_DIM, HEAD_DIM // ROPE_FRACTION) + \
        _rope_tables(s, IDX_DIM, IDX_DIM // ROPE_FRACTION)
    tok = lambda w: pl.BlockSpec((1, tm, w), lambda si, bi: (bi, si, 0))
    tab = pl.BlockSpec((tm, LANES), lambda si, bi: (si, 0))
    const = lambda shape: pl.BlockSpec(shape, lambda si, bi: (0,) * len(shape),
                                       pipeline_mode=pl.Buffered(1))
    out_shape = (
        jax.ShapeDtypeStruct((b, s, ATT_WIDTH), BF16),
        jax.ShapeDtypeStruct((b, s, KV_WIDTH), BF16),
        jax.ShapeDtypeStruct((b, s // kchunk, KV_WIDTH, kchunk), BF16),
        jax.ShapeDtypeStruct((b, s, IDX_WIDTH), BF16),
        jax.ShapeDtypeStruct((b, s, IDX_WIDTH), BF16),
        jax.ShapeDtypeStruct((b, s, 3 * CONV_CHANNELS), BF16),
        jax.ShapeDtypeStruct((b, s, CROSS_WIDTH), BF16),
        jax.ShapeDtypeStruct((b, s, LANES), F32),
    )
    out_specs = (tok(ATT_WIDTH), tok(KV_WIDTH),
                 pl.BlockSpec((1, tm // kchunk, KV_WIDTH, kchunk), lambda si, bi: (bi, si, 0, 0)),
                 tok(IDX_WIDTH), tok(IDX_WIDTH), tok(3 * CONV_CHANNELS), tok(CROSS_WIDTH), tok(LANES))
    return pl.pallas_call(
        functools.partial(_in_proj_kernel, kchunk=kchunk),
        grid=(s // tm, b),
        in_specs=[tok(d), const((1, d)), const((d, _W1_WIDTH))] + [tab] * 6,
        out_specs=out_specs,
        out_shape=out_shape,
        compiler_params=pltpu.CompilerParams(
            dimension_semantics=("arbitrary", "arbitrary"), vmem_limit_bytes=VMEM_LIMIT),
        name="in_proj",
    )(x, g_mix, w1, *tabs)


def _mem_kv_kernel(mem_ref, g_ref, w_ref, km_ref, vm_ref):
    h = _rms(mem_ref[0], g_ref[...]).astype(BF16)
    kv = _dot(h, w_ref[...])
    km_ref[0] = kv[:, :CROSS_WIDTH].astype(BF16)
    vm_ref[0] = kv[:, CROSS_WIDTH:].astype(BF16)


def _mem_kv(mem, g_mem, w_kv):
    b, m, d = mem.shape
    const = lambda shape: pl.BlockSpec(shape, lambda bi: (0,) * len(shape))
    spec = pl.BlockSpec((1, m, CROSS_WIDTH), lambda bi: (bi, 0, 0))
    return pl.pallas_call(
        _mem_kv_kernel,
        grid=(b,),
        in_specs=[pl.BlockSpec((1, m, d), lambda bi: (bi, 0, 0)), const((1, d)),
                  const((d, 2 * CROSS_WIDTH))],
        out_specs=(spec, spec),
        out_shape=(jax.ShapeDtypeStruct((b, m, CROSS_WIDTH), BF16),) * 2,
        compiler_params=pltpu.CompilerParams(dimension_semantics=("arbitrary",)),
        name="mem_kv",
    )(mem, g_mem, w_kv)


def _key_to_float(key):
    return lax.bitcast_convert_type(key ^ ((key >> 31) & 0x7FFFFFFF), F32)


def _col_reduce(x, op):
    rows, n = x.shape
    r = op(x.reshape(SUBLANES, rows // (SUBLANES * SUBLANES), SUBLANES, n), axis=1)
    return op(op(r, axis=0), axis=0, keepdims=True)


def _for_chunks(n, body, init, slab=False):
    def pair(i, carry):
        if slab:
            return body(2 * i, carry, 2)
        return body(2 * i + 1, body(2 * i, carry))
    carry = lax.fori_loop(0, n // 2, pair, init)
    return lax.cond(n % 2 == 1, lambda cr: body(n - 1, cr), lambda cr: cr, carry)


def _dsa_kernel(qi_ref, wi_ref, q_ref, ki_ref, ks_ref, vt_ref, o_ref, sc_ref, sh_ref, acc_ref,
                *, topk, jbits):
    tq = ch = Q_TILE
    batch = range(qi_ref.shape[0])
    qb = pl.program_id(1)
    nc = qb + 1
    qpos = qb * tq + lax.broadcasted_iota(I32, (1, tq), 1)

    def chunk_pos(c):
        off = pl.multiple_of(c * ch, ch)
        return off, off + lax.broadcasted_iota(I32, (ch, tq), 0)

    head_of_lane = lax.broadcasted_iota(I32, (tq, IDX_WIDTH), 1) // IDX_DIM
    zero_bf = jnp.zeros((), BF16)
    qi_heads = [[jnp.where(head_of_lane == h, qi_ref[bb], zero_bf) for h in range(N_IDX_HEADS)]
                for bb in batch]
    wit = [wi_ref[bb].T for bb in batch]
    wrow = [[wit[bb][h:h + 1, :] for h in range(N_IDX_HEADS)] for bb in batch]

    def scores(bb, c, nchunks=1):
        off = pl.multiple_of(c * ch, ch)
        kc = ki_ref[bb, pl.ds(off, nchunks * ch), :]
        acc = jnp.maximum(_dot_nt(kc, qi_heads[bb][0]), 0.0) * wrow[bb][0]
        for h in range(1, N_IDX_HEADS):
            acc = acc + jnp.maximum(_dot_nt(kc, qi_heads[bb][h]), 0.0) * wrow[bb][h]
        return off, acc

    def put_scores(bb, off, s):
        sc_ref[bb, pl.ds(off, s.shape[0]), :] = s
        hi = lax.bitcast_convert_type(lax.bitcast_convert_type(s, I32) & HI16_MASK, F32)
        sh_ref[bb, pl.ds(off, s.shape[0]), :] = hi.astype(BF16)

    def score_chunks(c, carry, nchunks=1):
        for bb in batch:
            put_scores(bb, *scores(bb, c, nchunks))
        return carry

    _for_chunks(qb, score_chunks, 0, slab=True)
    for bb in batch:
        off, acc = scores(bb, qb)
        put_scores(bb, off, jnp.where(chunk_pos(qb)[1] <= qpos, acc, -jnp.inf))

    def count_ge(ref, cands):
        rows = SUBLANES * (4 // ref.dtype.itemsize)
        one, zero = jnp.ones((), ref.dtype), jnp.zeros((), ref.dtype)

        def body(c, cnts):
            off, _ = chunk_pos(c)
            out = []
            for bb in batch:
                hit = jnp.where(ref[bb, pl.ds(off, ch), :] >= cands[bb], one, zero)
                parts = [hit[r:r + rows] for r in range(0, ch, rows)]
                while len(parts) > 1:
                    parts = [a + b for a, b in zip(parts[::2], parts[1::2])]
                out.append(cnts[bb] + parts[0].astype(F32))
            return tuple(out)

        cnts = _for_chunks(nc, body, tuple(jnp.zeros((rows, tq), F32) for _ in batch))
        return [cnt.sum(axis=0, keepdims=True) for cnt in cnts]

    def key_bit(i, carry, ref, to_cand):
        keys, n_keys = carry
        bit = lax.shift_left(jnp.int32(1), 31 - i)
        trials = [key + bit for key in keys]
        n_trials = count_ge(ref, [to_cand(trial) for trial in trials])
        keeps = [n_trial >= topk for n_trial in n_trials]
        return (tuple(jnp.where(keeps[bb], trials[bb], keys[bb]) for bb in batch),
                tuple(jnp.where(keeps[bb], n_trials[bb], n_keys[bb]) for bb in batch))

    def hi_cand(trial):
        pattern = trial ^ ((trial >> 31) & 0x7FFF0000)
        return lax.bitcast_convert_type(pattern, F32).astype(BF16)

    def count(pred):
        def body(c, cnts):
            off, kpos = chunk_pos(c)
            hits = [jnp.where(pred(bb, sc_ref[bb, pl.ds(off, ch), :], kpos), 1.0, 0.0) for bb in batch]
            return tuple(cnts[bb] + hits[bb].reshape(ch // SUBLANES, SUBLANES, tq).sum(axis=0)
                         for bb in batch)
        cnts = _for_chunks(nc, body, tuple(jnp.zeros((SUBLANES, tq), F32) for _ in batch))
        return [cnt.sum(axis=0, keepdims=True) for cnt in cnts]

    def search(carry):
        carry = lax.fori_loop(0, 16, lambda i, cr: key_bit(i, cr, sh_ref, hi_cand), carry)
        return lax.fori_loop(16, 32, lambda i, cr: key_bit(i, cr, sc_ref, _key_to_float), carry)

    init = (tuple(jnp.full((1, tq), INT_MIN, I32) for _ in batch),
            tuple(jnp.zeros((1, tq), F32) for _ in batch))
    keys, n_keys = lax.cond(nc * tq <= topk, lambda cr: cr, search, init)
    taus = [_key_to_float(jnp.maximum(key, KEY_MIN_FINITE)) for key in keys]

    excess = sum(jnp.where(n_key > topk, 1, 0) for n_key in n_keys)

    @pl.when(jnp.max(excess) > 0)
    def _():
        n_above = count(lambda bb, s, _: s > taus[bb])
        wants = [topk - n for n in n_above]

        def pos_bit(i, jmaxs):
            bit = lax.shift_left(jnp.int32(1), jbits - 1 - i)
            trials = [jmax + bit for jmax in jmaxs]
            n_before = count(lambda bb, s, kpos: jnp.where(s == taus[bb], kpos, trials[bb]) < trials[bb])
            return tuple(jnp.where(n_before[bb] <= wants[bb], trials[bb], jmaxs[bb]) for bb in batch)

        jmaxs = lax.fori_loop(0, jbits, pos_bit, tuple(jnp.zeros((1, tq), I32) for _ in batch))

        def drop(c, carry):
            off, kpos = chunk_pos(c)
            for bb in batch:
                s = sc_ref[bb, pl.ds(off, ch), :]
                late_tie = jnp.where(s == taus[bb], kpos, -1) >= jmaxs[bb]
                sc_ref[bb, pl.ds(off, ch), :] = jnp.where(late_tie, -jnp.inf, s)
            return carry

        lax.fori_loop(0, nc, drop, 0)

    group_of_lane = lax.broadcasted_iota(I32, (tq, LANES), 1) // HEAD_DIM
    qe = []
    for bb in batch:
        q = q_ref[bb]
        qe.append([])
        for h in range(N_ATT_HEADS):
            g = h // HEADS_PER_GROUP
            t = q[:, (h // 2) * LANES:(h // 2 + 1) * LANES].astype(F32)
            if h % 2 != g:
                t = pltpu.roll(t, HEAD_DIM, 1)
            qe[bb].append(jnp.where(group_of_lane == g, t, 0.0).astype(BF16))

    acc_ref[...] = jnp.zeros(acc_ref.shape, F32)
    streams = [(bb, h) for bb in batch for h in range(N_ATT_HEADS)]

    def attend(c, carry, nchunks=1):
        ms, ls = carry
        off = pl.multiple_of(c * ch, ch)
        rows = nchunks * ch
        kcs = [ks_ref[bb, pl.ds(off, rows), :] for bb in batch]
        lgs = [_dot_nt(kcs[bb], qe[bb][h]) for bb, h in streams]
        vts, biases = [], []
        for bb in batch:
            vt = vt_ref[bb, c]
            if nchunks == 2:
                vt = jnp.concatenate([vt, vt_ref[bb, c + 1]], axis=1)
            vts.append(vt)
            biases.append(jnp.where(sc_ref[bb, pl.ds(off, rows), :] >= taus[bb], 0.0, NEG_BIG))
        new_ms, new_ls = [], []
        for i, (bb, h) in enumerate(streams):
            lg = biases[bb] + lgs[i]
            m_new = jnp.maximum(ms[i], _col_reduce(lg, jnp.max))
            alpha = jnp.exp2(ms[i] - m_new)
            p = jnp.exp2(lg - m_new)
            new_ms.append(m_new)
            new_ls.append(alpha * ls[i] + _col_reduce(p, jnp.sum))
            cols = slice(h * tq, (h + 1) * tq)
            acc_ref[bb, :, cols] = alpha * acc_ref[bb, :, cols] + _dot(vts[bb], p.astype(BF16))
        return tuple(new_ms), tuple(new_ls)

    init = (tuple(jnp.full((1, tq), NEG_BIG, F32) for _ in streams),
            tuple(jnp.zeros((1, tq), F32) for _ in streams))
    _, ls = _for_chunks(nc, attend, init, slab=True)

    lane = lax.broadcasted_iota(I32, (tq, LANES), 1)
    for bb in batch:
        tiles = []
        for j in range(N_ATT_HEADS // 2):
            parts = []
            for p_ in range(2):
                h = 2 * j + p_
                blk = acc_ref[bb, :, h * tq:(h + 1) * tq] * (1.0 / ls[streams.index((bb, h))])
                if h // HEADS_PER_GROUP != p_:
                    blk = jnp.concatenate([blk[HEAD_DIM:], blk[:HEAD_DIM]], axis=0)
                parts.append(blk.T)
            tiles.append(jnp.where(lane < HEAD_DIM, parts[0], parts[1]))
        o_ref[bb] = jnp.concatenate(tiles, axis=1).astype(BF16)


def _dsa(qi, wi, q, ki, ks, vt):
    b, s, _ = q.shape
    topk = min(TOPK_MAX, s // 4)
    nb = DSA_BATCH if b % DSA_BATCH == 0 else 1
    blk = lambda w: pl.BlockSpec((nb, Q_TILE, w), lambda bi, qb: (bi, qb, 0))
    full = lambda w: pl.BlockSpec((nb, s, w), lambda bi, qb: (bi, 0, 0))
    return pl.pallas_call(
        functools.partial(_dsa_kernel, topk=topk, jbits=int(s).bit_length()),
        grid=(b // nb, s // Q_TILE),
        in_specs=[blk(IDX_WIDTH), blk(LANES), blk(ATT_WIDTH), full(IDX_WIDTH), full(KV_WIDTH),
                  pl.BlockSpec((nb, s // Q_TILE, KV_WIDTH, Q_TILE), lambda bi, qb: (bi, 0, 0, 0))],
        out_specs=blk(ATT_WIDTH),
        out_shape=jax.ShapeDtypeStruct((b, s, ATT_WIDTH), BF16),
        scratch_shapes=[pltpu.VMEM((nb, s, Q_TILE), F32), pltpu.VMEM((nb, s, Q_TILE), BF16),
                        pltpu.VMEM((nb, KV_WIDTH, N_ATT_HEADS * Q_TILE), F32)],
        compiler_params=pltpu.CompilerParams(
            dimension_semantics=("arbitrary", "arbitrary"), vmem_limit_bytes=VMEM_LIMIT),
        name="dsa",
    )(qi, wi, q, ki, ks, vt)


def _causal_conv3(u, w, carry_ref):
    prev = carry_ref[...]
    row = lax.broadcasted_iota(I32, prev.shape, 0)
    s1 = pltpu.roll(u, 1, 0)
    s2 = pltpu.roll(u, 2, 0)
    p1 = pltpu.roll(prev, 1, 0)
    p2 = pltpu.roll(prev, 2, 0)
    s1 = jnp.concatenate([jnp.where(row < 1, p1, s1[:SUBLANES]), s1[SUBLANES:]], axis=0)
    s2 = jnp.concatenate([jnp.where(row < 2, p2, s2[:SUBLANES]), s2[SUBLANES:]], axis=0)
    carry_ref[...] = u[u.shape[0] - SUBLANES:]
    return s2 * w[0:1] + s1 * w[1:2] + u * w[2:3]


def _merge_kernel(x_ref, g_ref, att_ref, conv_ref, qc_ref, km_ref, vm_ref, wg_ref, bg_ref, cw_ref,
                  wa_ref, wc_ref, wm_ref, wo_ref, o_ref, carry_ref):
    @pl.when(pl.program_id(1) == 0)
    def _():
        carry_ref[...] = jnp.zeros(carry_ref.shape, F32)

    x = x_ref[0]
    h = _rms(x, g_ref[...]).astype(BF16)

    y_att = _dot(att_ref[0], wa_ref[...])

    cin = conv_ref[0].astype(F32)
    c = CONV_CHANNELS
    cu = cin[:, c:2 * c] * cin[:, 2 * c:]
    yc = cin[:, :c] * _causal_conv3(cu, cw_ref[...], carry_ref)
    y_conv = _dot(yc.astype(BF16), wc_ref[...])

    qc, km, vm = qc_ref[0], km_ref[0], vm_ref[0]
    head_of_lane = lax.broadcasted_iota(I32, km.shape, 1) // HEAD_DIM
    zero_bf = jnp.zeros((), BF16)
    mem_out = jnp.zeros((x.shape[0], CROSS_WIDTH), F32)
    for hh in range(N_CROSS_HEADS):
        lg = _dot_nt(qc, jnp.where(head_of_lane == hh, km, zero_bf)) * (HEAD_DIM ** -0.5)
        e = jnp.exp(lg - lg.max(axis=-1, keepdims=True))
        p = e / e.sum(axis=-1, keepdims=True)
        mem_out = mem_out + _dot(p.astype(BF16), jnp.where(head_of_lane == hh, vm, zero_bf))
    y_mem = _dot(mem_out.astype(BF16), wm_ref[...])

    d = x.shape[1]
    merged = jnp.zeros(x.shape, F32)
    for i, y in enumerate((y_att, y_conv, y_mem)):
        pre = _dot(h, wg_ref[:, i * d:(i + 1) * d]) + bg_ref[:, i * d:(i + 1) * d]
        merged = merged + jax.nn.sigmoid(pre) * y
    o_ref[0] = x + _dot(merged.astype(BF16), wo_ref[...])


def _merge(x, g_mix, att, conv_in, qc, km, vm, wg, b_gate, conv_w, wa, wc, wm, wo, tm):
    b, s, d = x.shape
    m = km.shape[1]
    tok = lambda w: pl.BlockSpec((1, tm, w), lambda bi, si: (bi, si, 0))
    const = lambda shape: pl.BlockSpec(shape, lambda bi, si: (0,) * len(shape),
                                       pipeline_mode=pl.Buffered(1))
    memspec = pl.BlockSpec((1, m, CROSS_WIDTH), lambda bi, si: (bi, 0, 0))
    return pl.pallas_call(
        _merge_kernel,
        grid=(b, s // tm),
        in_specs=[tok(d), const((1, d)), tok(ATT_WIDTH), tok(3 * CONV_CHANNELS), tok(CROSS_WIDTH),
                  memspec, memspec, const((d, N_BRANCHES * d)), const((1, N_BRANCHES * d)),
                  const((CONV_K, CONV_CHANNELS)), const((ATT_WIDTH, d)), const((CONV_CHANNELS, d)),
                  const((CROSS_WIDTH, d)), const((d, d))],
        out_specs=tok(d),
        out_shape=jax.ShapeDtypeStruct((b, s, d), F32),
        scratch_shapes=[pltpu.VMEM((SUBLANES, CONV_CHANNELS), F32)],
        compiler_params=pltpu.CompilerParams(
            dimension_semantics=("arbitrary", "arbitrary"), vmem_limit_bytes=VMEM_LIMIT),
        name="merge",
    )(x, g_mix, att, conv_in, qc, km, vm, wg, b_gate, conv_w, wa, wc, wm, wo)


def _ffn_kernel(x_ref, g_ref, wu_ref, cw_ref, wd_ref, gf_ref, o_ref, carry_ref, act_ref, *, fc):
    @pl.when(pl.program_id(1) == 0)
    def _():
        carry_ref[...] = jnp.zeros(carry_ref.shape, F32)

    x = x_ref[0]
    h = _rms(x, g_ref[...]).astype(BF16)
    dff = wd_ref.shape[0]
    for c0 in range(0, dff, fc):
        halves = []
        for base in (0, dff):
            cols = slice(base + c0, base + c0 + fc)
            up = _dot(h, wu_ref[:, cols])
            halves.append(_causal_conv3(up, cw_ref[:, cols], carry_ref.at[:, cols]))
        gate, val = halves
        act_ref[:, c0:c0 + fc] = (gate * jax.nn.sigmoid(gate) * val).astype(BF16)
    y = x + _dot(act_ref[...], wd_ref[...])
    o_ref[0] = _rms(y, gf_ref[...])


def _ffn(x, g_ffn, wu, conv_w, wd, g_final, tm, fc):
    b, s, d = x.shape
    dff = wd.shape[0]
    tok = pl.BlockSpec((1, tm, d), lambda bi, si: (bi, si, 0))
    const = lambda shape: pl.BlockSpec(shape, lambda bi, si: (0,) * len(shape),
                                       pipeline_mode=pl.Buffered(1))
    return pl.pallas_call(
        functools.partial(_ffn_kernel, fc=fc),
        grid=(b, s // tm),
        in_specs=[tok, const((1, d)), const((d, 2 * dff)), const((CONV_K, 2 * dff)),
                  const((dff, d)), const((1, d))],
        out_specs=tok,
        out_shape=jax.ShapeDtypeStruct((b, s, d), F32),
        scratch_shapes=[pltpu.VMEM((SUBLANES, 2 * dff), F32), pltpu.VMEM((tm, dff), BF16)],
        compiler_params=pltpu.CompilerParams(
            dimension_semantics=("arbitrary", "arbitrary"), vmem_limit_bytes=VMEM_LIMIT),
        name="ffn",
    )(x, g_ffn, wu, conv_w, wd, g_final)


def _pack_in_proj_weight(w):
    sizes = (ATT_WIDTH, KV_WIDTH, KV_WIDTH, IDX_WIDTH, IDX_DIM, N_IDX_HEADS,
             3 * CONV_CHANNELS, CROSS_WIDTH)
    offs = np.concatenate([[0], np.cumsum(sizes)])
    wq, wk, wv, wqi, wki, wwi, wconv, wqc = [w[:, offs[i]:offs[i + 1]] for i in range(len(sizes))]
    idx_scale = (N_IDX_HEADS * IDX_DIM) ** -0.5
    att_scale = HEAD_DIM ** -0.5
    wwi = jnp.pad(wwi * idx_scale, ((0, 0), (0, LANES - N_IDX_HEADS)))
    w1 = jnp.concatenate([wq * att_scale, wwi, wk, wv, wqi, jnp.tile(wki, (1, N_IDX_HEADS)),
                          wconv, wqc], axis=1)
    return w1.astype(BF16), w[:, offs[-1]:].astype(BF16)


def _layer(x, mem, g_mix, w_in, b_gate, conv_w_short, w_att_out, w_conv_out, w_mem_out, w_o,
           g_mem, w_mem_kv, g_ffn, w_up, conv_w_ffn, w_down, g_final):
    b, s, d = x.shape
    tm1 = min(512, s)
    tm4 = min(512, s)
    tm5 = min(512, s)
    w1, wg = _pack_in_proj_weight(w_in)
    q, ks, vt, qi, ki, conv_in, qc, wi = _in_proj(x, g_mix[None], w1, tm1, Q_TILE)
    km, vm = _mem_kv(mem, g_mem[None], w_mem_kv.astype(BF16))
    att = _dsa(qi, wi, q, ki, ks, vt)
    x1 = _merge(x, g_mix[None], att, conv_in, qc, km, vm, wg, b_gate[None], conv_w_short,
                w_att_out.astype(BF16), w_conv_out.astype(BF16), w_mem_out.astype(BF16),
                w_o.astype(BF16), tm4)
    dff = w_down.shape[0]
    fc = 256 if dff % 256 == 0 else dff
    return _ffn(x1, g_ffn[None], w_up.astype(BF16), conv_w_ffn, w_down.astype(BF16),
                g_final[None], tm5, fc)


def kernel(x, mem, g_mix, w_in, b_gate, conv_w_short, w_att_out, w_conv_out, w_mem_out, w_o,
           g_mem, w_mem_kv, g_ffn, w_up, conv_w_ffn, w_down, g_final):
    depth = w_in.shape[0]
    assert depth == 1, "the final RMSNorm is fused into the last layer's ffn kernel"
    return _layer(x, mem, g_mix[0], w_in[0], b_gate[0], conv_w_short[0], w_att_out[0],
                  w_conv_out[0], w_mem_out[0], w_o[0], g_mem[0], w_mem_kv[0], g_ffn[0],
                  w_up[0], conv_w_ffn[0], w_down[0], g_final)
```

```python
import functools

import numpy as np
import jax
import jax.numpy as jnp
from jax import lax
from jax.experimental import pallas as pl
from jax.experimental.pallas import tpu as pltpu

F32 = jnp.float32
BF16 = jnp.bfloat16
I32 = jnp.int32

HEAD_DIM = 64
N_ATT_HEADS = 6
N_KV_GROUPS = 2
HEADS_PER_GROUP = N_ATT_HEADS // N_KV_GROUPS
ATT_WIDTH = N_ATT_HEADS * HEAD_DIM
KV_WIDTH = N_KV_GROUPS * HEAD_DIM
N_IDX_HEADS = 8
IDX_DIM = 32
IDX_WIDTH = N_IDX_HEADS * IDX_DIM
TOPK_MAX = 256
CONV_CHANNELS = 384
CONV_K = 3
N_CROSS_HEADS = 4
CROSS_WIDTH = N_CROSS_HEADS * HEAD_DIM
N_BRANCHES = 3
ROPE_THETA = 500000.0
ROPE_FRACTION = 4
NORM_EPS = 1e-6

LANES = 128
SUBLANES = 8
VMEM_LIMIT = 56 * 1024 * 1024

TOKEN_TILE = 512
FFN_CHUNK = 256
Q_TILE = 256
DSA_BATCH = 4
NEG_BIG = -1e30
LOG2_E = 1.4426950408889634
INT_MIN = -2 ** 31
KEY_MIN_FINITE = -2139095040
HI16_MASK = -65536


def _rms(x, g):
    return x * lax.rsqrt(jnp.mean(x * x, axis=-1, keepdims=True) + NORM_EPS) * g


def _dot_nt(a, b):
    return lax.dot_general(a, b, (((1,), (1,)), ((), ())), preferred_element_type=F32)


def _dot(a, b):
    return jnp.dot(a, b, preferred_element_type=F32)


def _rope_tile(x, c, sa, sb, half):
    return x * c + pltpu.roll(x, LANES - half, 1) * sa + pltpu.roll(x, half, 1) * sb


def _rope(x, c, sa, sb, half):
    tiles = [_rope_tile(x[:, i:i + LANES], c, sa, sb, half) for i in range(0, x.shape[1], LANES)]
    return tiles[0] if len(tiles) == 1 else jnp.concatenate(tiles, axis=1)


_OFF_Q = 0
_OFF_WI = _OFF_Q + ATT_WIDTH
_OFF_KV = _OFF_WI + LANES
_OFF_QI = _OFF_KV + 2 * KV_WIDTH
_OFF_KI = _OFF_QI + IDX_WIDTH
_OFF_CONV = _OFF_KI + IDX_WIDTH
_OFF_QC = _OFF_CONV + 3 * CONV_CHANNELS
_W1_WIDTH = _OFF_QC + CROSS_WIDTH


def _in_proj_kernel(x_ref, g_ref, w_ref, ca_ref, saa_ref, sba_ref, ci_ref, sai_ref, sbi_ref,
                    q_ref, ks_ref, vt_ref, qi_ref, ki_ref, conv_ref, qc_ref, wi_ref, *, kchunk):
    h = _rms(x_ref[0], g_ref[...]).astype(BF16)
    ca, saa, sba = ca_ref[...], saa_ref[...], sba_ref[...]
    ci, sai, sbi = ci_ref[...], sai_ref[...], sbi_ref[...]
    half_a = HEAD_DIM // ROPE_FRACTION // 2
    half_i = IDX_DIM // ROPE_FRACTION // 2

    a = _dot(h, w_ref[:, _OFF_Q:_OFF_KV])
    q_ref[0] = (_rope(a[:, :ATT_WIDTH], ca, saa, sba, half_a) * LOG2_E).astype(BF16)
    wi_ref[0] = a[:, ATT_WIDTH:]

    kv = _dot(h, w_ref[:, _OFF_KV:_OFF_QI])
    ks_ref[0] = _rope(kv[:, :KV_WIDTH], ca, saa, sba, half_a).astype(BF16)
    v = kv[:, KV_WIDTH:]
    for i in range(v.shape[0] // kchunk):
        vt_ref[0, i] = v[i * kchunk:(i + 1) * kchunk].T.astype(BF16)

    idx = _dot(h, w_ref[:, _OFF_QI:_OFF_CONV])
    qi_ref[0] = _rope(idx[:, :IDX_WIDTH], ci, sai, sbi, half_i).astype(BF16)
    ki_ref[0] = _rope(idx[:, IDX_WIDTH:], ci, sai, sbi, half_i).astype(BF16)

    cq = _dot(h, w_ref[:, _OFF_CONV:_W1_WIDTH])
    conv_ref[0] = cq[:, :3 * CONV_CHANNELS].astype(BF16)
    qc_ref[0] = cq[:, 3 * CONV_CHANNELS:].astype(BF16)


def _rope_tables(s, period, rot_dim):
    half = rot_dim // 2
    inv_freq = ROPE_THETA ** (-jnp.arange(half, dtype=F32) / half)
    ang = jnp.arange(s).astype(F32)[:, None] * inv_freq[None, :]
    cos, sin = jnp.cos(ang), jnp.sin(ang)
    j = np.arange(LANES) % period
    first = jnp.asarray(j < half)[None, :]
    second = jnp.asarray((j >= half) & (j < rot_dim))[None, :]
    fidx = np.where(j < half, j, np.where(j < rot_dim, j - half, 0))
    cos_l, sin_l = cos[:, fidx], sin[:, fidx]
    c = jnp.where(first | second, cos_l, 1.0)
    sa = jnp.where(first, -sin_l, 0.0)
    sb = jnp.where(second, sin_l, 0.0)
    return c, sa, sb


def _in_proj(x, g_mix, w1, tm, kchunk):
    b, s, d = x.shape
    tabs = _rope_tables(s, HEAD_DIM, HEAD_DIM // ROPE_FRACTION) + \
        _rope_tables(s, IDX_DIM, IDX_DIM // ROPE_FRACTION)
    tok = lambda w: pl.BlockSpec((1, tm, w), lambda si, bi: (bi, si, 0))
    tab = pl.BlockSpec((tm, LANES), lambda si, bi: (si, 0))
    const = lambda shape: pl.BlockSpec(shape, lambda si, bi: (0,) * len(shape),
                                       pipeline_mode=pl.Buffered(1))
    out_shape = (
        jax.ShapeDtypeStruct((b, s, ATT_WIDTH), BF16),
        jax.ShapeDtypeStruct((b, s, KV_WIDTH), BF16),
        jax.ShapeDtypeStruct((b, s // kchunk, KV_WIDTH, kchunk), BF16),
        jax.ShapeDtypeStruct((b, s, IDX_WIDTH), BF16),
        jax.ShapeDtypeStruct((b, s, IDX_WIDTH), BF16),
        jax.ShapeDtypeStruct((b, s, 3 * CONV_CHANNELS), BF16),
        jax.ShapeDtypeStruct((b, s, CROSS_WIDTH), BF16),
        jax.ShapeDtypeStruct((b, s, LANES), F32),
    )
    out_specs = (tok(ATT_WIDTH), tok(KV_WIDTH),
                 pl.BlockSpec((1, tm // kchunk, KV_WIDTH, kchunk), lambda si, bi: (bi, si, 0, 0)),
                 tok(IDX_WIDTH), tok(IDX_WIDTH), tok(3 * CONV_CHANNELS), tok(CROSS_WIDTH), tok(LANES))
    return pl.pallas_call(
        functools.partial(_in_proj_kernel, kchunk=kchunk),
        grid=(s // tm, b),
        in_specs=[tok(d), const((1, d)), const((d, _W1_WIDTH))] + [tab] * 6,
        out_specs=out_specs,
        out_shape=out_shape,
        compiler_params=pltpu.CompilerParams(
            dimension_semantics=("arbitrary", "arbitrary"), vmem_limit_bytes=VMEM_LIMIT),
        name="in_proj",
    )(x, g_mix, w1, *tabs)


def _mem_kv_kernel(mem_ref, g_ref, w_ref, km_ref, vm_ref):
    h = _rms(mem_ref[0], g_ref[...]).astype(BF16)
    kv = _dot(h, w_ref[...])
    km_ref[0] = kv[:, :CROSS_WIDTH].astype(BF16)
    vm_ref[0] = kv[:, CROSS_WIDTH:].astype(BF16)


def _mem_kv(mem, g_mem, w_kv):
    b, m, d = mem.shape
    const = lambda shape: pl.BlockSpec(shape, lambda bi: (0,) * len(shape))
    spec = pl.BlockSpec((1, m, CROSS_WIDTH), lambda bi: (bi, 0, 0))
    return pl.pallas_call(
        _mem_kv_kernel,
        grid=(b,),
        in_specs=[pl.BlockSpec((1, m, d), lambda bi: (bi, 0, 0)), const((1, d)),
                  const((d, 2 * CROSS_WIDTH))],
        out_specs=(spec, spec),
        out_shape=(jax.ShapeDtypeStruct((b, m, CROSS_WIDTH), BF16),) * 2,
        compiler_params=pltpu.CompilerParams(dimension_semantics=("arbitrary",)),
        name="mem_kv",
    )(mem, g_mem, w_kv)


def _key_to_float(key):
    return lax.bitcast_convert_type(key ^ ((key >> 31) & 0x7FFFFFFF), F32)


def _col_reduce(x, op):
    rows, n = x.shape
    r = op(x.reshape(SUBLANES, rows // (SUBLANES * SUBLANES), SUBLANES, n), axis=1)
    return op(op(r, axis=0), axis=0, keepdims=True)


def _for_chunks(n, body, init, slab=False):
    def pair(i, carry):
        if slab:
            return body(2 * i, carry, 2)
        return body(2 * i + 1, body(2 * i, carry))
    carry = lax.fori_loop(0, n // 2, pair, init)
    return lax.cond(n % 2 == 1, lambda cr: body(n - 1, cr), lambda cr: cr, carry)


def _dsa_kernel(qi_ref, wi_ref, q_ref, ki_ref, ks_ref, vt_ref, o_ref, sc_ref, sh_ref, acc_ref,
                *, topk, jbits):
    tq = ch = Q_TILE
    batch = range(qi_ref.shape[0])
    qb = pl.program_id(1)
    nc = qb + 1
    qpos = qb * tq + lax.broadcasted_iota(I32, (1, tq), 1)

    def chunk_pos(c):
        off = pl.multiple_of(c * ch, ch)
        return off, off + lax.broadcasted_iota(I32, (ch, tq), 0)

    head_of_lane = lax.broadcasted_iota(I32, (tq, IDX_WIDTH), 1) // IDX_DIM
    zero_bf = jnp.zeros((), BF16)
    qi_heads = [[jnp.where(head_of_lane == h, qi_ref[bb], zero_bf) for h in range(N_IDX_HEADS)]
                for bb in batch]
    wit = [wi_ref[bb].T for bb in batch]
    wrow = [[wit[bb][h:h + 1, :] for h in range(N_IDX_HEADS)] for bb in batch]

    def scores(bb, c, nchunks=1):
        off = pl.multiple_of(c * ch, ch)
        kc = ki_ref[bb, pl.ds(off, nchunks * ch), :]
        acc = jnp.maximum(_dot_nt(kc, qi_heads[bb][0]), 0.0) * wrow[bb][0]
        for h in range(1, N_IDX_HEADS):
            acc = acc + jnp.maximum(_dot_nt(kc, qi_heads[bb][h]), 0.0) * wrow[bb][h]
        return off, acc

    def put_scores(bb, off, s):
        sc_ref[bb, pl.ds(off, s.shape[0]), :] = s
        hi = lax.bitcast_convert_type(lax.bitcast_convert_type(s, I32) & HI16_MASK, F32)
        sh_ref[bb, pl.ds(off, s.shape[0]), :] = hi.astype(BF16)

    def score_chunks(c, carry, nchunks=1):
        for bb in batch:
            put_scores(bb, *scores(bb, c, nchunks))
        return carry

    _for_chunks(qb, score_chunks, 0, slab=True)
    for bb in batch:
        off, acc = scores(bb, qb)
        put_scores(bb, off, jnp.where(chunk_pos(qb)[1] <= qpos, acc, -jnp.inf))

    def count_ge(ref, cands):
        rows = SUBLANES * (4 // ref.dtype.itemsize)
        one, zero = jnp.ones((), ref.dtype), jnp.zeros((), ref.dtype)

        def body(c, cnts):
            off, _ = chunk_pos(c)
            out = []
            for bb in batch:
                hit = jnp.where(ref[bb, pl.ds(off, ch), :] >= cands[bb], one, zero)
                parts = [hit[r:r + rows] for r in range(0, ch, rows)]
                while len(parts) > 1:
                    parts = [a + b for a, b in zip(parts[::2], parts[1::2])]
                out.append(cnts[bb] + parts[0].astype(F32))
            return tuple(out)

        cnts = _for_chunks(nc, body, tuple(jnp.zeros((rows, tq), F32) for _ in batch))
        return [cnt.sum(axis=0, keepdims=True) for cnt in cnts]

    def key_bit(i, carry, ref, to_cand):
        keys, n_keys = carry
        bit = lax.shift_left(jnp.int32(1), 31 - i)
        trials = [key + bit for key in keys]
        n_trials = count_ge(ref, [to_cand(trial) for trial in trials])
        keeps = [n_trial >= topk for n_trial in n_trials]
        return (tuple(jnp.where(keeps[bb], trials[bb], keys[bb]) for bb in batch),
                tuple(jnp.where(keeps[bb], n_trials[bb], n_keys[bb]) for bb in batch))

    def hi_cand(trial):
        pattern = trial ^ ((trial >> 31) & 0x7FFF0000)
        return lax.bitcast_convert_type(pattern, F32).astype(BF16)

    def count(pred):
        def body(c, cnts):
            off, kpos = chunk_pos(c)
            hits = [jnp.where(pred(bb, sc_ref[bb, pl.ds(off, ch), :], kpos), 1.0, 0.0) for bb in batch]
            return tuple(cnts[bb] + hits[bb].reshape(ch // SUBLANES, SUBLANES, tq).sum(axis=0)
                         for bb in batch)
        cnts = _for_chunks(nc, body, tuple(jnp.zeros((SUBLANES, tq), F32) for _ in batch))
        return [cnt.sum(axis=0, keepdims=True) for cnt in cnts]

    def search(carry):
        carry = lax.fori_loop(0, 16, lambda i, cr: key_bit(i, cr, sh_ref, hi_cand), carry)
        return lax.fori_loop(16, 32, lambda i, cr: key_bit(i, cr, sc_ref, _key_to_float), carry)

    init = (tuple(jnp.full((1, tq), INT_MIN, I32) for _ in batch),
            tuple(jnp.zeros((1, tq), F32) for _ in batch))
    keys, n_keys = lax.cond(nc * tq <= topk, lambda cr: cr, search, init)
    taus = [_key_to_float(jnp.maximum(key, KEY_MIN_FINITE)) for key in keys]

    excess = sum(jnp.where(n_key > topk, 1, 0) for n_key in n_keys)

    @pl.when(jnp.max(excess) > 0)
    def _():
        n_above = count(lambda bb, s, _: s > taus[bb])
        wants = [topk - n for n in n_above]

        def pos_bit(i, jmaxs):
            bit = lax.shift_left(jnp.int32(1), jbits - 1 - i)
            trials = [jmax + bit for jmax in jmaxs]
            n_before = count(lambda bb, s, kpos: jnp.where(s == taus[bb], kpos, trials[bb]) < trials[bb])
            return tuple(jnp.where(n_before[bb] <= wants[bb], trials[bb], jmaxs[bb]) for bb in batch)

        jmaxs = lax.fori_loop(0, jbits, pos_bit, tuple(jnp.zeros((1, tq), I32) for _ in batch))

        def drop(c, carry):
            off, kpos = chunk_pos(c)
            for bb in batch:
                s = sc_ref[bb, pl.ds(off, ch), :]
                late_tie = jnp.where(s == taus[bb], kpos, -1) >= jmaxs[bb]
                sc_ref[bb, pl.ds(off, ch), :] = jnp.where(late_tie, -jnp.inf, s)
            return carry

        lax.fori_loop(0, nc, drop, 0)

    group_of_lane = lax.broadcasted_iota(I32, (tq, LANES), 1) // HEAD_DIM
    qe = []
    for bb in batch:
        q = q_ref[bb]
        qe.append([])
        for h in range(N_ATT_HEADS):
            g = h // HEADS_PER_GROUP
            t = q[:, (h // 2) * LANES:(h // 2 + 1) * LANES].astype(F32)
            if h % 2 != g:
                t = pltpu.roll(t, HEAD_DIM, 1)
            qe[bb].append(jnp.where(group_of_lane == g, t, 0.0).astype(BF16))

    acc_ref[...] = jnp.zeros(acc_ref.shape, F32)
    streams = [(bb, h) for bb in batch for h in range(N_ATT_HEADS)]

    def attend(c, carry, nchunks=1):
        ms, ls = carry
        off = pl.multiple_of(c * ch, ch)
        rows = nchunks * ch
        kcs = [ks_ref[bb, pl.ds(off, rows), :] for bb in batch]
        lgs = [_dot_nt(kcs[bb], qe[bb][h]) for bb, h in streams]
        vts, biases = [], []
        for bb in batch:
            vt = vt_ref[bb, c]
            if nchunks == 2:
                vt = jnp.concatenate([vt, vt_ref[bb, c + 1]], axis=1)
            vts.append(vt)
            biases.append(jnp.where(sc_ref[bb, pl.ds(off, rows), :] >= taus[bb], 0.0, NEG_BIG))
        new_ms, new_ls = [], []
        for i, (bb, h) in enumerate(streams):
            lg = biases[bb] + lgs[i]
            m_new = jnp.maximum(ms[i], _col_reduce(lg, jnp.max))
            alpha = jnp.exp2(ms[i] - m_new)
            p = jnp.exp2(lg - m_new)
            new_ms.append(m_new)
            new_ls.append(alpha * ls[i] + _col_reduce(p, jnp.sum))
            cols = slice(h * tq, (h + 1) * tq)
            acc_ref[bb, :, cols] = alpha * acc_ref[bb, :, cols] + _dot(vts[bb], p.astype(BF16))
        return tuple(new_ms), tuple(new_ls)

    init = (tuple(jnp.full((1, tq), NEG_BIG, F32) for _ in streams),
            tuple(jnp.zeros((1, tq), F32) for _ in streams))
    _, ls = _for_chunks(nc, attend, init, slab=True)

    lane = lax.broadcasted_iota(I32, (tq, LANES), 1)
    for bb in batch:
        tiles = []
        for j in range(N_ATT_HEADS // 2):
            parts = []
            for p_ in range(2):
                h = 2 * j + p_
                blk = acc_ref[bb, :, h * tq:(h + 1) * tq] * (1.0 / ls[streams.index((bb, h))])
                if h // HEADS_PER_GROUP != p_:
                    blk = jnp.concatenate([blk[HEAD_DIM:], blk[:HEAD_DIM]], axis=0)
                parts.append(blk.T)
            tiles.append(jnp.where(lane < HEAD_DIM, parts[0], parts[1]))
        o_ref[bb] = jnp.concatenate(tiles, axis=1).astype(BF16)


def _dsa(qi, wi, q, ki, ks, vt):
    b, s, _ = q.shape
    topk = min(TOPK_MAX, s // 4)
    nb = DSA_BATCH if b % DSA_BATCH == 0 else 1
    blk = lambda w: pl.BlockSpec((nb, Q_TILE, w), lambda bi, qb: (bi, qb, 0))
    full = lambda w: pl.BlockSpec((nb, s, w), lambda bi, qb: (bi, 0, 0))
    return pl.pallas_call(
        functools.partial(_dsa_kernel, topk=topk, jbits=int(s).bit_length()),
        grid=(b // nb, s // Q_TILE),
        in_specs=[blk(IDX_WIDTH), blk(LANES), blk(ATT_WIDTH), full(IDX_WIDTH), full(KV_WIDTH),
                  pl.BlockSpec((nb, s // Q_TILE, KV_WIDTH, Q_TILE), lambda bi, qb: (bi, 0, 0, 0))],
        out_specs=blk(ATT_WIDTH),
        out_shape=jax.ShapeDtypeStruct((b, s, ATT_WIDTH), BF16),
        scratch_shapes=[pltpu.VMEM((nb, s, Q_TILE), F32), pltpu.VMEM((nb, s, Q_TILE), BF16),
                        pltpu.VMEM((nb, KV_WIDTH, N_ATT_HEADS * Q_TILE), F32)],
        compiler_params=pltpu.CompilerParams(
            dimension_semantics=("arbitrary", "arbitrary"), vmem_limit_bytes=VMEM_LIMIT),
        name="dsa",
    )(qi, wi, q, ki, ks, vt)


def _causal_conv3(u, w, carry_ref):
    prev = carry_ref[...]
    row = lax.broadcasted_iota(I32, prev.shape, 0)
    s1 = pltpu.roll(u, 1, 0)
    s2 = pltpu.roll(u, 2, 0)
    p1 = pltpu.roll(prev, 1, 0)
    p2 = pltpu.roll(prev, 2, 0)
    s1 = jnp.concatenate([jnp.where(row < 1, p1, s1[:SUBLANES]), s1[SUBLANES:]], axis=0)
    s2 = jnp.concatenate([jnp.where(row < 2, p2, s2[:SUBLANES]), s2[SUBLANES:]], axis=0)
    carry_ref[...] = u[u.shape[0] - SUBLANES:]
    return s2 * w[0:1] + s1 * w[1:2] + u * w[2:3]


def _merge_kernel(x_ref, g_ref, att_ref, conv_ref, qc_ref, km_ref, vm_ref, wg_ref, bg_ref, cw_ref,
                  wa_ref, wc_ref, wm_ref, wo_ref, o_ref, carry_ref):
    @pl.when(pl.program_id(1) == 0)
    def _():
        carry_ref[...] = jnp.zeros(carry_ref.shape, F32)

    x = x_ref[0]
    h = _rms(x, g_ref[...]).astype(BF16)

    y_att = _dot(att_ref[0], wa_ref[...])

    cin = conv_ref[0].astype(F32)
    c = CONV_CHANNELS
    cu = cin[:, c:2 * c] * cin[:, 2 * c:]
    yc = cin[:, :c] * _causal_conv3(cu, cw_ref[...], carry_ref)
    y_conv = _dot(yc.astype(BF16), wc_ref[...])

    qc, km, vm = qc_ref[0], km_ref[0], vm_ref[0]
    head_of_lane = lax.broadcasted_iota(I32, km.shape, 1) // HEAD_DIM
    zero_bf = jnp.zeros((), BF16)
    mem_out = jnp.zeros((x.shape[0], CROSS_WIDTH), F32)
    for hh in range(N_CROSS_HEADS):
        lg = _dot_nt(qc, jnp.where(head_of_lane == hh, km, zero_bf)) * (HEAD_DIM ** -0.5)
        e = jnp.exp(lg - lg.max(axis=-1, keepdims=True))
        p = e / e.sum(axis=-1, keepdims=True)
        mem_out = mem_out + _dot(p.astype(BF16), jnp.where(head_of_lane == hh, vm, zero_bf))
    y_mem = _dot(mem_out.astype(BF16), wm_ref[...])

    d = x.shape[1]
    merged = jnp.zeros(x.shape, F32)
    for i, y in enumerate((y_att, y_conv, y_mem)):
        pre = bg_ref[:, i * d:(i + 1) * d] + _dot(h, wg_ref[:, i * d:(i + 1) * d])
        merged = merged + jax.nn.sigmoid(pre) * y
    o_ref[0] = x + _dot(merged.astype(BF16), wo_ref[...])


def _merge(x, g_mix, att, conv_in, qc, km, vm, wg, b_gate, conv_w, wa, wc, wm, wo, tm):
    b, s, d = x.shape
    m = km.shape[1]
    tok = lambda w: pl.BlockSpec((1, tm, w), lambda bi, si: (bi, si, 0))
    const = lambda shape: pl.BlockSpec(shape, lambda bi, si: (0,) * len(shape),
                                       pipeline_mode=pl.Buffered(1))
    memspec = pl.BlockSpec((1, m, CROSS_WIDTH), lambda bi, si: (bi, 0, 0))
    return pl.pallas_call(
        _merge_kernel,
        grid=(b, s // tm),
        in_specs=[tok(d), const((1, d)), tok(ATT_WIDTH), tok(3 * CONV_CHANNELS), tok(CROSS_WIDTH),
                  memspec, memspec, const((d, N_BRANCHES * d)), const((1, N_BRANCHES * d)),
                  const((CONV_K, CONV_CHANNELS)), const((ATT_WIDTH, d)), const((CONV_CHANNELS, d)),
                  const((CROSS_WIDTH, d)), const((d, d))],
        out_specs=tok(d),
        out_shape=jax.ShapeDtypeStruct((b, s, d), F32),
        scratch_shapes=[pltpu.VMEM((SUBLANES, CONV_CHANNELS), F32)],
        compiler_params=pltpu.CompilerParams(
            dimension_semantics=("arbitrary", "arbitrary"), vmem_limit_bytes=VMEM_LIMIT),
        name="merge",
    )(x, g_mix, att, conv_in, qc, km, vm, wg, b_gate, conv_w, wa, wc, wm, wo)


def _ffn_kernel(x_ref, g_ref, wu_ref, cw_ref, wd_ref, gf_ref, o_ref, carry_ref, act_ref, *, fc):
    @pl.when(pl.program_id(1) == 0)
    def _():
        carry_ref[...] = jnp.zeros(carry_ref.shape, F32)

    x = x_ref[0]
    h = _rms(x, g_ref[...]).astype(BF16)
    dff = wd_ref.shape[0]
    for c0 in range(0, dff, fc):
        halves = []
        for base in (0, dff):
            cols = slice(base + c0, base + c0 + fc)
            up = _dot(h, wu_ref[:, cols])
            halves.append(_causal_conv3(up, cw_ref[:, cols], carry_ref.at[:, cols]))
        gate, val = halves
        act_ref[:, c0:c0 + fc] = (gate * jax.nn.sigmoid(gate) * val).astype(BF16)
    y = x + _dot(act_ref[...], wd_ref[...])
    o_ref[0] = _rms(y, gf_ref[...])


def _ffn(x, g_ffn, wu, conv_w, wd, g_final, tm, fc):
    b, s, d = x.shape
    dff = wd.shape[0]
    tok = pl.BlockSpec((1, tm, d), lambda bi, si: (bi, si, 0))
    const = lambda shape: pl.BlockSpec(shape, lambda bi, si: (0,) * len(shape),
                                       pipeline_mode=pl.Buffered(1))
    return pl.pallas_call(
        functools.partial(_ffn_kernel, fc=fc),
        grid=(b, s // tm),
        in_specs=[tok, const((1, d)), const((d, 2 * dff)), const((CONV_K, 2 * dff)),
                  const((dff, d)), const((1, d))],
        out_specs=tok,
        out_shape=jax.ShapeDtypeStruct((b, s, d), F32),
        scratch_shapes=[pltpu.VMEM((SUBLANES, 2 * dff), F32), pltpu.VMEM((tm, dff), BF16)],
        compiler_params=pltpu.CompilerParams(
            dimension_semantics=("arbitrary", "arbitrary"), vmem_limit_bytes=VMEM_LIMIT),
        name="ffn",
    )(x, g_ffn, wu, conv_w, wd, g_final)


def _pack_in_proj_weight(w):
    sizes = (ATT_WIDTH, KV_WIDTH, KV_WIDTH, IDX_WIDTH, IDX_DIM, N_IDX_HEADS,
             3 * CONV_CHANNELS, CROSS_WIDTH)
    offs = np.concatenate([[0], np.cumsum(sizes)])
    wq, wk, wv, wqi, wki, wwi, wconv, wqc = [w[:, offs[i]:offs[i + 1]] for i in range(len(sizes))]
    idx_scale = (N_IDX_HEADS * IDX_DIM) ** -0.5
    att_scale = HEAD_DIM ** -0.5
    wwi = jnp.pad(wwi * idx_scale, ((0, 0), (0, LANES - N_IDX_HEADS)))
    w1 = jnp.concatenate([wq * att_scale, wwi, wk, wv, wqi, jnp.tile(wki, (1, N_IDX_HEADS)),
                          wconv, wqc], axis=1)
    return w1.astype(BF16), w[:, offs[-1]:].astype(BF16)


def _layer(x, mem, g_mix, w_in, b_gate, conv_w_short, w_att_out, w_conv_out, w_mem_out, w_o,
           g_mem, w_mem_kv, g_ffn, w_up, conv_w_ffn, w_down, g_final):
    b, s, d = x.shape
    tm = min(TOKEN_TILE, s)
    assert s % tm == 0 and tm % Q_TILE == 0 and d % LANES == 0, (s, d)
    w1, wg = _pack_in_proj_weight(w_in)
    q, ks, vt, qi, ki, conv_in, qc, wi = _in_proj(x, g_mix[None], w1, tm, Q_TILE)
    km, vm = _mem_kv(mem, g_mem[None], w_mem_kv.astype(BF16))
    att = _dsa(qi, wi, q, ki, ks, vt)
    x1 = _merge(x, g_mix[None], att, conv_in, qc, km, vm, wg, b_gate[None], conv_w_short,
                w_att_out.astype(BF16), w_conv_out.astype(BF16), w_mem_out.astype(BF16),
                w_o.astype(BF16), tm)
    dff = w_down.shape[0]
    fc = FFN_CHUNK if dff % FFN_CHUNK == 0 else dff
    return _ffn(x1, g_ffn[None], w_up.astype(BF16), conv_w_ffn, w_down.astype(BF16),
                g_final[None], tm, fc)


def kernel(x, mem, g_mix, w_in, b_gate, conv_w_short, w_att_out, w_conv_out, w_mem_out, w_o,
           g_mem, w_mem_kv, g_ffn, w_up, conv_w_ffn, w_down, g_final):
    depth = w_in.shape[0]
    assert depth == 1, "the final RMSNorm is fused into the last layer's ffn kernel"
    return _layer(x, mem, g_mix[0], w_in[0], b_gate[0], conv_w_short[0], w_att_out[0],
                  w_conv_out[0], w_mem_out[0], w_o[0], g_mem[0], w_mem_kv[0], g_ffn[0],
                  w_up[0], conv_w_ffn[0], w_down[0], g_final)
```

```python
import functools

import numpy as np
import jax
import jax.numpy as jnp
from jax import lax
from jax.experimental import pallas as pl
from jax.experimental.pallas import tpu as pltpu

F32 = jnp.float32
BF16 = jnp.bfloat16
I32 = jnp.int32

HEAD_DIM = 64
N_ATT_HEADS = 6
N_KV_GROUPS = 2
HEADS_PER_GROUP = N_ATT_HEADS // N_KV_GROUPS
ATT_WIDTH = N_ATT_HEADS * HEAD_DIM
KV_WIDTH = N_KV_GROUPS * HEAD_DIM
N_IDX_HEADS = 8
IDX_DIM = 32
IDX_WIDTH = N_IDX_HEADS * IDX_DIM
TOPK_MAX = 256
CONV_CHANNELS = 384
CONV_K = 3
N_CROSS_HEADS = 4
CROSS_WIDTH = N_CROSS_HEADS * HEAD_DIM
N_BRANCHES = 3
ROPE_THETA = 500000.0
ROPE_FRACTION = 4
NORM_EPS = 1e-6

LANES = 128
SUBLANES = 8
VMEM_LIMIT = 56 * 1024 * 1024

TOKEN_TILE = 1024
FFN_TILE = 512
FFN_CHUNK = 256
Q_TILE = 256
DSA_BATCH = 4
NEG_BIG = -1e30
LOG2_E = 1.4426950408889634
INT_MIN = -2 ** 31
KEY_MIN_FINITE = -2139095040
HI16_MASK = -65536


def _rms(x, g):
    return x * lax.rsqrt(jnp.mean(x * x, axis=-1, keepdims=True) + NORM_EPS) * g


def _dot_nt(a, b):
    return lax.dot_general(a, b, (((1,), (1,)), ((), ())), preferred_element_type=F32)


def _dot(a, b):
    return jnp.dot(a, b, preferred_element_type=F32)


def _rope_tile(x, c, sa, sb, half):
    return x * c + pltpu.roll(x, LANES - half, 1) * sa + pltpu.roll(x, half, 1) * sb


def _rope(x, c, sa, sb, half):
    tiles = [_rope_tile(x[:, i:i + LANES], c, sa, sb, half) for i in range(0, x.shape[1], LANES)]
    return tiles[0] if len(tiles) == 1 else jnp.concatenate(tiles, axis=1)


_OFF_Q = 0
_OFF_WI = _OFF_Q + ATT_WIDTH
_OFF_KV = _OFF_WI + LANES
_OFF_QI = _OFF_KV + 2 * KV_WIDTH
_OFF_KI = _OFF_QI + IDX_WIDTH
_OFF_CONV = _OFF_KI + IDX_WIDTH
_OFF_QC = _OFF_CONV + 3 * CONV_CHANNELS
_W1_WIDTH = _OFF_QC + CROSS_WIDTH


def _in_proj_kernel(x_ref, g_ref, w_ref, ca_ref, saa_ref, sba_ref, ci_ref, sai_ref, sbi_ref,
                    q_ref, ks_ref, vt_ref, qi_ref, ki_ref, conv_ref, qc_ref, wi_ref, *, kchunk):
    h = _rms(x_ref[0], g_ref[...]).astype(BF16)
    ca, saa, sba = ca_ref[...], saa_ref[...], sba_ref[...]
    ci, sai, sbi = ci_ref[...], sai_ref[...], sbi_ref[...]
    half_a = HEAD_DIM // ROPE_FRACTION // 2
    half_i = IDX_DIM // ROPE_FRACTION // 2

    a = _dot(h, w_ref[:, _OFF_Q:_OFF_KV])
    q_ref[0] = (_rope(a[:, :ATT_WIDTH], ca, saa, sba, half_a) * LOG2_E).astype(BF16)
    wi_ref[0] = a[:, ATT_WIDTH:]

    kv = _dot(h, w_ref[:, _OFF_KV:_OFF_QI])
    ks_ref[0] = _rope(kv[:, :KV_WIDTH], ca, saa, sba, half_a).astype(BF16)
    v = kv[:, KV_WIDTH:]
    for i in range(v.shape[0] // kchunk):
        vt_ref[0, i] = v[i * kchunk:(i + 1) * kchunk].T.astype(BF16)

    idx = _dot(h, w_ref[:, _OFF_QI:_OFF_CONV])
    qi_ref[0] = _rope(idx[:, :IDX_WIDTH], ci, sai, sbi, half_i).astype(BF16)
    ki_ref[0] = _rope(idx[:, IDX_WIDTH:], ci, sai, sbi, half_i).astype(BF16)

    cq = _dot(h, w_ref[:, _OFF_CONV:_W1_WIDTH])
    conv_ref[0] = cq[:, :3 * CONV_CHANNELS].astype(BF16)
    qc_ref[0] = cq[:, 3 * CONV_CHANNELS:].astype(BF16)


def _rope_tables(s, period, rot_dim):
    half = rot_dim // 2
    inv_freq = ROPE_THETA ** (-jnp.arange(half, dtype=F32) / half)
    ang = jnp.arange(s).astype(F32)[:, None] * inv_freq[None, :]
    cos, sin = jnp.cos(ang), jnp.sin(ang)
    j = np.arange(LANES) % period
    first = jnp.asarray(j < half)[None, :]
    second = jnp.asarray((j >= half) & (j < rot_dim))[None, :]
    fidx = np.where(j < half, j, np.where(j < rot_dim, j - half, 0))
    cos_l, sin_l = cos[:, fidx], sin[:, fidx]
    c = jnp.where(first | second, cos_l, 1.0)
    sa = jnp.where(first, -sin_l, 0.0)
    sb = jnp.where(second, sin_l, 0.0)
    return c, sa, sb


def _in_proj(x, g_mix, w1, tm, kchunk):
    b, s, d = x.shape
    tabs = _rope_tables(s, HEAD_DIM, HEAD_DIM // ROPE_FRACTION) + \
        _rope_tables(s, IDX_DIM, IDX_DIM // ROPE_FRACTION)
    tok = lambda w: pl.BlockSpec((1, tm, w), lambda si, bi: (bi, si, 0))
    tab = pl.BlockSpec((tm, LANES), lambda si, bi: (si, 0))
    const = lambda shape: pl.BlockSpec(shape, lambda si, bi: (0,) * len(shape),
                                       pipeline_mode=pl.Buffered(1))
    out_shape = (
        jax.ShapeDtypeStruct((b, s, ATT_WIDTH), BF16),
        jax.ShapeDtypeStruct((b, s, KV_WIDTH), BF16),
        jax.ShapeDtypeStruct((b, s // kchunk, KV_WIDTH, kchunk), BF16),
        jax.ShapeDtypeStruct((b, s, IDX_WIDTH), BF16),
        jax.ShapeDtypeStruct((b, s, IDX_WIDTH), BF16),
        jax.ShapeDtypeStruct((b, s, 3 * CONV_CHANNELS), BF16),
        jax.ShapeDtypeStruct((b, s, CROSS_WIDTH), BF16),
        jax.ShapeDtypeStruct((b, s, LANES), F32),
    )
    out_specs = (tok(ATT_WIDTH), tok(KV_WIDTH),
                 pl.BlockSpec((1, tm // kchunk, KV_WIDTH, kchunk), lambda si, bi: (bi, si, 0, 0)),
                 tok(IDX_WIDTH), tok(IDX_WIDTH), tok(3 * CONV_CHANNELS), tok(CROSS_WIDTH), tok(LANES))
    return pl.pallas_call(
        functools.partial(_in_proj_kernel, kchunk=kchunk),
        grid=(s // tm, b),
        in_specs=[tok(d), const((1, d)), const((d, _W1_WIDTH))] + [tab] * 6,
        out_specs=out_specs,
        out_shape=out_shape,
        compiler_params=pltpu.CompilerParams(
            dimension_semantics=("arbitrary", "arbitrary"), vmem_limit_bytes=VMEM_LIMIT),
        name="in_proj",
    )(x, g_mix, w1, *tabs)


def _mem_kv_kernel(mem_ref, g_ref, w_ref, km_ref, vm_ref):
    h = _rms(mem_ref[0], g_ref[...]).astype(BF16)
    kv = _dot(h, w_ref[...])
    km_ref[0] = kv[:, :CROSS_WIDTH].astype(BF16)
    vm_ref[0] = kv[:, CROSS_WIDTH:].astype(BF16)


def _mem_kv(mem, g_mem, w_kv):
    b, m, d = mem.shape
    const = lambda shape: pl.BlockSpec(shape, lambda bi: (0,) * len(shape))
    spec = pl.BlockSpec((1, m, CROSS_WIDTH), lambda bi: (bi, 0, 0))
    return pl.pallas_call(
        _mem_kv_kernel,
        grid=(b,),
        in_specs=[pl.BlockSpec((1, m, d), lambda bi: (bi, 0, 0)), const((1, d)),
                  const((d, 2 * CROSS_WIDTH))],
        out_specs=(spec, spec),
        out_shape=(jax.ShapeDtypeStruct((b, m, CROSS_WIDTH), BF16),) * 2,
        compiler_params=pltpu.CompilerParams(dimension_semantics=("arbitrary",)),
        name="mem_kv",
    )(mem, g_mem, w_kv)


def _key_to_float(key):
    return lax.bitcast_convert_type(key ^ ((key >> 31) & 0x7FFFFFFF), F32)


def _col_reduce(x, op):
    rows, n = x.shape
    r = op(x.reshape(SUBLANES, rows // (SUBLANES * SUBLANES), SUBLANES, n), axis=1)
    return op(op(r, axis=0), axis=0, keepdims=True)


def _for_chunks(n, body, init, slab=False):
    def pair(i, carry):
        if slab:
            return body(2 * i, carry, 2)
        return body(2 * i + 1, body(2 * i, carry))
    carry = lax.fori_loop(0, n // 2, pair, init)
    return lax.cond(n % 2 == 1, lambda cr: body(n - 1, cr), lambda cr: cr, carry)


def _dsa_kernel(qi_ref, wi_ref, q_ref, ki_ref, ks_ref, vt_ref, o_ref, sc_ref, sh_ref, acc_ref,
                *, topk, jbits):
    tq = ch = Q_TILE
    batch = range(qi_ref.shape[0])
    qb = pl.program_id(1)
    nc = qb + 1
    qpos = qb * tq + lax.broadcasted_iota(I32, (1, tq), 1)

    def chunk_pos(c):
        off = pl.multiple_of(c * ch, ch)
        return off, off + lax.broadcasted_iota(I32, (ch, tq), 0)

    head_of_lane = lax.broadcasted_iota(I32, (tq, IDX_WIDTH), 1) // IDX_DIM
    zero_bf = jnp.zeros((), BF16)
    qi_heads = [[jnp.where(head_of_lane == h, qi_ref[bb], zero_bf) for h in range(N_IDX_HEADS)]
                for bb in batch]
    wit = [wi_ref[bb].T for bb in batch]
    wrow = [[wit[bb][h:h + 1, :] for h in range(N_IDX_HEADS)] for bb in batch]

    def scores(bb, c, nchunks=1):
        off = pl.multiple_of(c * ch, ch)
        kc = ki_ref[bb, pl.ds(off, nchunks * ch), :]
        acc = jnp.maximum(_dot_nt(kc, qi_heads[bb][0]), 0.0) * wrow[bb][0]
        for h in range(1, N_IDX_HEADS):
            acc = acc + jnp.maximum(_dot_nt(kc, qi_heads[bb][h]), 0.0) * wrow[bb][h]
        return off, acc

    def put_scores(bb, off, s):
        sc_ref[bb, pl.ds(off, s.shape[0]), :] = s
        hi = lax.bitcast_convert_type(lax.bitcast_convert_type(s, I32) & HI16_MASK, F32)
        sh_ref[bb, pl.ds(off, s.shape[0]), :] = hi.astype(BF16)

    def score_chunks(c, carry, nchunks=1):
        for bb in batch:
            put_scores(bb, *scores(bb, c, nchunks))
        return carry

    _for_chunks(qb, score_chunks, 0, slab=True)
    for bb in batch:
        off, acc = scores(bb, qb)
        put_scores(bb, off, jnp.where(chunk_pos(qb)[1] <= qpos, acc, -jnp.inf))

    def count_ge(ref, cands):
        rows = SUBLANES * (4 // ref.dtype.itemsize)
        one, zero = jnp.ones((), ref.dtype), jnp.zeros((), ref.dtype)

        def body(c, cnts):
            off, _ = chunk_pos(c)
            out = []
            for bb in batch:
                hit = jnp.where(ref[bb, pl.ds(off, ch), :] >= cands[bb], one, zero)
                parts = [hit[r:r + rows] for r in range(0, ch, rows)]
                while len(parts) > 1:
                    parts = [a + b for a, b in zip(parts[::2], parts[1::2])]
                out.append(cnts[bb] + parts[0].astype(F32))
            return tuple(out)

        cnts = _for_chunks(nc, body, tuple(jnp.zeros((rows, tq), F32) for _ in batch))
        return [cnt.sum(axis=0, keepdims=True) for cnt in cnts]

    def key_bit(i, carry, ref, to_cand):
        keys, n_keys = carry
        bit = lax.shift_left(jnp.int32(1), 31 - i)
        trials = [key + bit for key in keys]
        n_trials = count_ge(ref, [to_cand(trial) for trial in trials])
        keeps = [n_trial >= topk for n_trial in n_trials]
        return (tuple(jnp.where(keeps[bb], trials[bb], keys[bb]) for bb in batch),
                tuple(jnp.where(keeps[bb], n_trials[bb], n_keys[bb]) for bb in batch))

    def hi_cand(trial):
        pattern = trial ^ ((trial >> 31) & 0x7FFF0000)
        return lax.bitcast_convert_type(pattern, F32).astype(BF16)

    def count(pred):
        def body(c, cnts):
            off, kpos = chunk_pos(c)
            hits = [jnp.where(pred(bb, sc_ref[bb, pl.ds(off, ch), :], kpos), 1.0, 0.0) for bb in batch]
            return tuple(cnts[bb] + hits[bb].reshape(ch // SUBLANES, SUBLANES, tq).sum(axis=0)
                         for bb in batch)
        cnts = _for_chunks(nc, body, tuple(jnp.zeros((SUBLANES, tq), F32) for _ in batch))
        return [cnt.sum(axis=0, keepdims=True) for cnt in cnts]

    def search(carry):
        carry = lax.fori_loop(0, 16, lambda i, cr: key_bit(i, cr, sh_ref, hi_cand), carry)
        return lax.fori_loop(16, 32, lambda i, cr: key_bit(i, cr, sc_ref, _key_to_float), carry)

    init = (tuple(jnp.full((1, tq), INT_MIN, I32) for _ in batch),
            tuple(jnp.zeros((1, tq), F32) for _ in batch))
    keys, n_keys = lax.cond(nc * tq <= topk, lambda cr: cr, search, init)
    taus = [_key_to_float(jnp.maximum(key, KEY_MIN_FINITE)) for key in keys]

    excess = sum(jnp.where(n_key > topk, 1, 0) for n_key in n_keys)

    @pl.when(jnp.max(excess) > 0)
    def _():
        n_above = count(lambda bb, s, _: s > taus[bb])
        wants = [topk - n for n in n_above]

        def pos_bit(i, jmaxs):
            bit = lax.shift_left(jnp.int32(1), jbits - 1 - i)
            trials = [jmax + bit for jmax in jmaxs]
            n_before = count(lambda bb, s, kpos: jnp.where(s == taus[bb], kpos, trials[bb]) < trials[bb])
            return tuple(jnp.where(n_before[bb] <= wants[bb], trials[bb], jmaxs[bb]) for bb in batch)

        jmaxs = lax.fori_loop(0, jbits, pos_bit, tuple(jnp.zeros((1, tq), I32) for _ in batch))

        def drop(c, carry):
            off, kpos = chunk_pos(c)
            for bb in batch:
                s = sc_ref[bb, pl.ds(off, ch), :]
                late_tie = jnp.where(s == taus[bb], kpos, -1) >= jmaxs[bb]
                sc_ref[bb, pl.ds(off, ch), :] = jnp.where(late_tie, -jnp.inf, s)
            return carry

        lax.fori_loop(0, nc, drop, 0)

    group_of_lane = lax.broadcasted_iota(I32, (tq, LANES), 1) // HEAD_DIM
    qe = []
    for bb in batch:
        q = q_ref[bb]
        qe.append([])
        for h in range(N_ATT_HEADS):
            g = h // HEADS_PER_GROUP
            t = q[:, (h // 2) * LANES:(h // 2 + 1) * LANES].astype(F32)
            if h % 2 != g:
                t = pltpu.roll(t, HEAD_DIM, 1)
            qe[bb].append(jnp.where(group_of_lane == g, t, 0.0).astype(BF16))

    acc_ref[...] = jnp.zeros(acc_ref.shape, F32)
    streams = [(bb, h) for bb in batch for h in range(N_ATT_HEADS)]

    def attend(c, carry, nchunks=1):
        ms, ls = carry
        off = pl.multiple_of(c * ch, ch)
        rows = nchunks * ch
        kcs = [ks_ref[bb, pl.ds(off, rows), :] for bb in batch]
        lgs = [_dot_nt(kcs[bb], qe[bb][h]) for bb, h in streams]
        vts, biases = [], []
        for bb in batch:
            vt = vt_ref[bb, c]
            if nchunks == 2:
                vt = jnp.concatenate([vt, vt_ref[bb, c + 1]], axis=1)
            vts.append(vt)
            biases.append(jnp.where(sc_ref[bb, pl.ds(off, rows), :] >= taus[bb], 0.0, NEG_BIG))
        new_ms, new_ls = [], []
        for i, (bb, h) in enumerate(streams):
            lg = biases[bb] + lgs[i]
            m_new = jnp.maximum(ms[i], _col_reduce(lg, jnp.max))
            alpha = jnp.exp2(ms[i] - m_new)
            p = jnp.exp2(lg - m_new)
            new_ms.append(m_new)
            new_ls.append(alpha * ls[i] + _col_reduce(p, jnp.sum))
            cols = slice(h * tq, (h + 1) * tq)
            acc_ref[bb, :, cols] = alpha * acc_ref[bb, :, cols] + _dot(vts[bb], p.astype(BF16))
        return tuple(new_ms), tuple(new_ls)

    init = (tuple(jnp.full((1, tq), NEG_BIG, F32) for _ in streams),
            tuple(jnp.zeros((1, tq), F32) for _ in streams))
    _, ls = _for_chunks(nc, attend, init, slab=True)

    lane = lax.broadcasted_iota(I32, (tq, LANES), 1)
    for bb in batch:
        tiles = []
        for j in range(N_ATT_HEADS // 2):
            parts = []
            for p_ in range(2):
                h = 2 * j + p_
                blk = acc_ref[bb, :, h * tq:(h + 1) * tq] * (1.0 / ls[streams.index((bb, h))])
                if h // HEADS_PER_GROUP != p_:
                    blk = jnp.concatenate([blk[HEAD_DIM:], blk[:HEAD_DIM]], axis=0)
                parts.append(blk.T)
            tiles.append(jnp.where(lane < HEAD_DIM, parts[0], parts[1]))
        o_ref[bb] = jnp.concatenate(tiles, axis=1).astype(BF16)


def _dsa(qi, wi, q, ki, ks, vt):
    b, s, _ = q.shape
    topk = min(TOPK_MAX, s // 4)
    nb = DSA_BATCH if b % DSA_BATCH == 0 else 1
    blk = lambda w: pl.BlockSpec((nb, Q_TILE, w), lambda bi, qb: (bi, qb, 0))
    full = lambda w: pl.BlockSpec((nb, s, w), lambda bi, qb: (bi, 0, 0))
    return pl.pallas_call(
        functools.partial(_dsa_kernel, topk=topk, jbits=int(s).bit_length()),
        grid=(b // nb, s // Q_TILE),
        in_specs=[blk(IDX_WIDTH), blk(LANES), blk(ATT_WIDTH), full(IDX_WIDTH), full(KV_WIDTH),
                  pl.BlockSpec((nb, s // Q_TILE, KV_WIDTH, Q_TILE), lambda bi, qb: (bi, 0, 0, 0))],
        out_specs=blk(ATT_WIDTH),
        out_shape=jax.ShapeDtypeStruct((b, s, ATT_WIDTH), BF16),
        scratch_shapes=[pltpu.VMEM((nb, s, Q_TILE), F32), pltpu.VMEM((nb, s, Q_TILE), BF16),
                        pltpu.VMEM((nb, KV_WIDTH, N_ATT_HEADS * Q_TILE), F32)],
        compiler_params=pltpu.CompilerParams(
            dimension_semantics=("arbitrary", "arbitrary"), vmem_limit_bytes=VMEM_LIMIT),
        name="dsa",
    )(qi, wi, q, ki, ks, vt)


def _causal_conv3(u, w, carry_ref):
    prev = carry_ref[...]
    row = lax.broadcasted_iota(I32, prev.shape, 0)
    s1 = pltpu.roll(u, 1, 0)
    s2 = pltpu.roll(u, 2, 0)
    p1 = pltpu.roll(prev, 1, 0)
    p2 = pltpu.roll(prev, 2, 0)
    s1 = jnp.concatenate([jnp.where(row < 1, p1, s1[:SUBLANES]), s1[SUBLANES:]], axis=0)
    s2 = jnp.concatenate([jnp.where(row < 2, p2, s2[:SUBLANES]), s2[SUBLANES:]], axis=0)
    carry_ref[...] = u[u.shape[0] - SUBLANES:]
    return s2 * w[0:1] + s1 * w[1:2] + u * w[2:3]


def _merge_kernel(x_ref, g_ref, att_ref, conv_ref, qc_ref, km_ref, vm_ref, wg_ref, bg_ref, cw_ref,
                  wa_ref, wc_ref, wm_ref, wo_ref, o_ref, carry_ref):
    @pl.when(pl.program_id(1) == 0)
    def _():
        carry_ref[...] = jnp.zeros(carry_ref.shape, F32)

    x = x_ref[0]
    h = _rms(x, g_ref[...]).astype(BF16)

    y_att = _dot(att_ref[0], wa_ref[...])

    cin = conv_ref[0].astype(F32)
    c = CONV_CHANNELS
    cu = cin[:, c:2 * c] * cin[:, 2 * c:]
    yc = cin[:, :c] * _causal_conv3(cu, cw_ref[...], carry_ref)
    y_conv = _dot(yc.astype(BF16), wc_ref[...])

    qc, km, vm = qc_ref[0], km_ref[0], vm_ref[0]
    head_of_lane = lax.broadcasted_iota(I32, km.shape, 1) // HEAD_DIM
    zero_bf = jnp.zeros((), BF16)
    mem_out = jnp.zeros((x.shape[0], CROSS_WIDTH), F32)
    for hh in range(N_CROSS_HEADS):
        lg = _dot_nt(qc, jnp.where(head_of_lane == hh, km, zero_bf)) * (HEAD_DIM ** -0.5)
        e = jnp.exp(lg - lg.max(axis=-1, keepdims=True))
        p = e / e.sum(axis=-1, keepdims=True)
        mem_out = mem_out + _dot(p.astype(BF16), jnp.where(head_of_lane == hh, vm, zero_bf))
    y_mem = _dot(mem_out.astype(BF16), wm_ref[...])

    d = x.shape[1]
    merged = jnp.zeros(x.shape, F32)
    for i, y in enumerate((y_att, y_conv, y_mem)):
        pre = bg_ref[:, i * d:(i + 1) * d] + _dot(h, wg_ref[:, i * d:(i + 1) * d])
        merged = merged + jax.nn.sigmoid(pre) * y
    o_ref[0] = x + _dot(merged.astype(BF16), wo_ref[...])


def _merge(x, g_mix, att, conv_in, qc, km, vm, wg, b_gate, conv_w, wa, wc, wm, wo, tm):
    b, s, d = x.shape
    m = km.shape[1]
    tok = lambda w: pl.BlockSpec((1, tm, w), lambda bi, si: (bi, si, 0))
    const = lambda shape: pl.BlockSpec(shape, lambda bi, si: (0,) * len(shape),
                                       pipeline_mode=pl.Buffered(1))
    memspec = pl.BlockSpec((1, m, CROSS_WIDTH), lambda bi, si: (bi, 0, 0))
    return pl.pallas_call(
        _merge_kernel,
        grid=(b, s // tm),
        in_specs=[tok(d), const((1, d)), tok(ATT_WIDTH), tok(3 * CONV_CHANNELS), tok(CROSS_WIDTH),
                  memspec, memspec, const((d, N_BRANCHES * d)), const((1, N_BRANCHES * d)),
                  const((CONV_K, CONV_CHANNELS)), const((ATT_WIDTH, d)), const((CONV_CHANNELS, d)),
                  const((CROSS_WIDTH, d)), const((d, d))],
        out_specs=tok(d),
        out_shape=jax.ShapeDtypeStruct((b, s, d), F32),
        scratch_shapes=[pltpu.VMEM((SUBLANES, CONV_CHANNELS), F32)],
        compiler_params=pltpu.CompilerParams(
            dimension_semantics=("arbitrary", "arbitrary"), vmem_limit_bytes=VMEM_LIMIT),
        name="merge",
    )(x, g_mix, att, conv_in, qc, km, vm, wg, b_gate, conv_w, wa, wc, wm, wo)


def _ffn_kernel(x_ref, g_ref, wu_ref, cw_ref, wd_ref, gf_ref, o_ref, carry_ref, act_ref, *, fc):
    @pl.when(pl.program_id(1) == 0)
    def _():
        carry_ref[...] = jnp.zeros(carry_ref.shape, F32)

    x = x_ref[0]
    h = _rms(x, g_ref[...]).astype(BF16)
    dff = wd_ref.shape[0]
    for c0 in range(0, dff, fc):
        halves = []
        for base in (0, dff):
            cols = slice(base + c0, base + c0 + fc)
            up = _dot(h, wu_ref[:, cols])
            halves.append(_causal_conv3(up, cw_ref[:, cols], carry_ref.at[:, cols]))
        gate, val = halves
        act_ref[:, c0:c0 + fc] = (gate * jax.nn.sigmoid(gate) * val).astype(BF16)
    y = x + _dot(act_ref[...], wd_ref[...])
    o_ref[0] = _rms(y, gf_ref[...])


def _ffn(x, g_ffn, wu, conv_w, wd, g_final, tm, fc):
    b, s, d = x.shape
    dff = wd.shape[0]
    tok = pl.BlockSpec((1, tm, d), lambda bi, si: (bi, si, 0))
    const = lambda shape: pl.BlockSpec(shape, lambda bi, si: (0,) * len(shape),
                                       pipeline_mode=pl.Buffered(1))
    return pl.pallas_call(
        functools.partial(_ffn_kernel, fc=fc),
        grid=(b, s // tm),
        in_specs=[tok, const((1, d)), const((d, 2 * dff)), const((CONV_K, 2 * dff)),
                  const((dff, d)), const((1, d))],
        out_specs=tok,
        out_shape=jax.ShapeDtypeStruct((b, s, d), F32),
        scratch_shapes=[pltpu.VMEM((SUBLANES, 2 * dff), F32), pltpu.VMEM((tm, dff), BF16)],
        compiler_params=pltpu.CompilerParams(
            dimension_semantics=("arbitrary", "arbitrary"), vmem_limit_bytes=VMEM_LIMIT),
        name="ffn",
    )(x, g_ffn, wu, conv_w, wd, g_final)


def _pack_in_proj_weight(w):
    sizes = (ATT_WIDTH, KV_WIDTH, KV_WIDTH, IDX_WIDTH, IDX_DIM, N_IDX_HEADS,
             3 * CONV_CHANNELS, CROSS_WIDTH)
    offs = np.concatenate([[0], np.cumsum(sizes)])
    wq, wk, wv, wqi, wki, wwi, wconv, wqc = [w[:, offs[i]:offs[i + 1]] for i in range(len(sizes))]
    idx_scale = (N_IDX_HEADS * IDX_DIM) ** -0.5
    att_scale = HEAD_DIM ** -0.5
    wwi = jnp.pad(wwi * idx_scale, ((0, 0), (0, LANES - N_IDX_HEADS)))
    w1 = jnp.concatenate([wq * att_scale, wwi, wk, wv, wqi, jnp.tile(wki, (1, N_IDX_HEADS)),
                          wconv, wqc], axis=1)
    return w1.astype(BF16), w[:, offs[-1]:].astype(BF16)


def _layer(x, mem, g_mix, w_in, b_gate, conv_w_short, w_att_out, w_conv_out, w_mem_out, w_o,
           g_mem, w_mem_kv, g_ffn, w_up, conv_w_ffn, w_down, g_final):
    b, s, d = x.shape
    tm, tm_ffn = min(TOKEN_TILE, s), min(FFN_TILE, s)
    assert s % tm == 0 and s % tm_ffn == 0 and tm % Q_TILE == 0 and d % LANES == 0, (s, d)
    w1, wg = _pack_in_proj_weight(w_in)
    q, ks, vt, qi, ki, conv_in, qc, wi = _in_proj(x, g_mix[None], w1, tm, Q_TILE)
    km, vm = _mem_kv(mem, g_mem[None], w_mem_kv.astype(BF16))
    att = _dsa(qi, wi, q, ki, ks, vt)
    x1 = _merge(x, g_mix[None], att, conv_in, qc, km, vm, wg, b_gate[None], conv_w_short,
                w_att_out.astype(BF16), w_conv_out.astype(BF16), w_mem_out.astype(BF16),
                w_o.astype(BF16), tm)
    dff = w_down.shape[0]
    fc = FFN_CHUNK if dff % FFN_CHUNK == 0 else dff
    return _ffn(x1, g_ffn[None], w_up.astype(BF16), conv_w_ffn, w_down.astype(BF16),
                g_final[None], tm_ffn, fc)


def kernel(x, mem, g_mix, w_in, b_gate, conv_w_short, w_att_out, w_conv_out, w_mem_out, w_o,
           g_mem, w_mem_kv, g_ffn, w_up, conv_w_ffn, w_down, g_final):
    depth = w_in.shape[0]
    assert depth == 1, "the final RMSNorm is fused into the last layer's ffn kernel"
    return _layer(x, mem, g_mix[0], w_in[0], b_gate[0], conv_w_short[0], w_att_out[0],
                  w_conv_out[0], w_mem_out[0], w_o[0], g_mem[0], w_mem_kv[0], g_ffn[0],
                  w_up[0], conv_w_ffn[0], w_down[0], g_final)
```

```python
import functools

import numpy as np
import jax
import jax.numpy as jnp
from jax import lax
from jax.experimental import pallas as pl
from jax.experimental.pallas import tpu as pltpu

F32 = jnp.float32
BF16 = jnp.bfloat16
I32 = jnp.int32

HEAD_DIM = 64
N_ATT_HEADS = 6
N_KV_GROUPS = 2
HEADS_PER_GROUP = N_ATT_HEADS // N_KV_GROUPS
ATT_WIDTH = N_ATT_HEADS * HEAD_DIM
KV_WIDTH = N_KV_GROUPS * HEAD_DIM
N_IDX_HEADS = 8
IDX_DIM = 32
IDX_WIDTH = N_IDX_HEADS * IDX_DIM
TOPK_MAX = 256
CONV_CHANNELS = 384
CONV_K = 3
N_CROSS_HEADS = 4
CROSS_WIDTH = N_CROSS_HEADS * HEAD_DIM
N_BRANCHES = 3
ROPE_THETA = 500000.0
ROPE_FRACTION = 4
NORM_EPS = 1e-6

LANES = 128
SUBLANES = 8
VMEM_LIMIT = 56 * 1024 * 1024

TOKEN_TILE = 1024
FFN_TILE = 1024
FFN_CHUNK = 256
Q_TILE = 256
DSA_BATCH = 4
NEG_BIG = -1e30
LOG2_E = 1.4426950408889634
INT_MIN = -2 ** 31
KEY_MIN_FINITE = -2139095040
HI16_MASK = -65536


def _rms(x, g):
    return x * lax.rsqrt(jnp.mean(x * x, axis=-1, keepdims=True) + NORM_EPS) * g


def _dot_nt(a, b):
    return lax.dot_general(a, b, (((1,), (1,)), ((), ())), preferred_element_type=F32)


def _dot(a, b):
    return jnp.dot(a, b, preferred_element_type=F32)


def _rope_tile(x, c, sa, sb, half):
    return x * c + pltpu.roll(x, LANES - half, 1) * sa + pltpu.roll(x, half, 1) * sb


def _rope(x, c, sa, sb, half):
    tiles = [_rope_tile(x[:, i:i + LANES], c, sa, sb, half) for i in range(0, x.shape[1], LANES)]
    return tiles[0] if len(tiles) == 1 else jnp.concatenate(tiles, axis=1)


_OFF_Q = 0
_OFF_WI = _OFF_Q + ATT_WIDTH
_OFF_KV = _OFF_WI + LANES
_OFF_QI = _OFF_KV + 2 * KV_WIDTH
_OFF_KI = _OFF_QI + IDX_WIDTH
_OFF_CONV = _OFF_KI + IDX_WIDTH
_OFF_QC = _OFF_CONV + 3 * CONV_CHANNELS
_W1_WIDTH = _OFF_QC + CROSS_WIDTH


def _in_proj_kernel(x_ref, g_ref, w_ref, ca_ref, saa_ref, sba_ref, ci_ref, sai_ref, sbi_ref,
                    q_ref, ks_ref, vt_ref, qi_ref, ki_ref, conv_ref, qc_ref, wi_ref, *, kchunk):
    h = _rms(x_ref[0], g_ref[...]).astype(BF16)
    ca, saa, sba = ca_ref[...], saa_ref[...], sba_ref[...]
    ci, sai, sbi = ci_ref[...], sai_ref[...], sbi_ref[...]
    half_a = HEAD_DIM // ROPE_FRACTION // 2
    half_i = IDX_DIM // ROPE_FRACTION // 2

    a = _dot(h, w_ref[:, _OFF_Q:_OFF_KV])
    q_ref[0] = (_rope(a[:, :ATT_WIDTH], ca, saa, sba, half_a) * LOG2_E).astype(BF16)
    wi_ref[0] = a[:, ATT_WIDTH:]

    kv = _dot(h, w_ref[:, _OFF_KV:_OFF_QI])
    ks_ref[0] = _rope(kv[:, :KV_WIDTH], ca, saa, sba, half_a).astype(BF16)
    v = kv[:, KV_WIDTH:]
    for i in range(v.shape[0] // kchunk):
        vt_ref[0, i] = v[i * kchunk:(i + 1) * kchunk].T.astype(BF16)

    idx = _dot(h, w_ref[:, _OFF_QI:_OFF_CONV])
    qi_ref[0] = _rope(idx[:, :IDX_WIDTH], ci, sai, sbi, half_i).astype(BF16)
    ki_ref[0] = _rope(idx[:, IDX_WIDTH:], ci, sai, sbi, half_i).astype(BF16)

    cq = _dot(h, w_ref[:, _OFF_CONV:_W1_WIDTH])
    conv_ref[0] = cq[:, :3 * CONV_CHANNELS].astype(BF16)
    qc_ref[0] = cq[:, 3 * CONV_CHANNELS:].astype(BF16)


def _rope_tables(s, period, rot_dim):
    half = rot_dim // 2
    inv_freq = ROPE_THETA ** (-jnp.arange(half, dtype=F32) / half)
    ang = jnp.arange(s).astype(F32)[:, None] * inv_freq[None, :]
    cos, sin = jnp.cos(ang), jnp.sin(ang)
    j = np.arange(LANES) % period
    first = jnp.asarray(j < half)[None, :]
    second = jnp.asarray((j >= half) & (j < rot_dim))[None, :]
    fidx = np.where(j < half, j, np.where(j < rot_dim, j - half, 0))
    cos_l, sin_l = cos[:, fidx], sin[:, fidx]
    c = jnp.where(first | second, cos_l, 1.0)
    sa = jnp.where(first, -sin_l, 0.0)
    sb = jnp.where(second, sin_l, 0.0)
    return c, sa, sb


def _in_proj(x, g_mix, w1, tm, kchunk):
    b, s, d = x.shape
    tabs = _rope_tables(s, HEAD_DIM, HEAD_DIM // ROPE_FRACTION) + \
        _rope_tables(s, IDX_DIM, IDX_DIM // ROPE_FRACTION)
    tok = lambda w: pl.BlockSpec((1, tm, w), lambda si, bi: (bi, si, 0))
    tab = pl.BlockSpec((tm, LANES), lambda si, bi: (si, 0))
    const = lambda shape: pl.BlockSpec(shape, lambda si, bi: (0,) * len(shape),
                                       pipeline_mode=pl.Buffered(1))
    out_shape = (
        jax.ShapeDtypeStruct((b, s, ATT_WIDTH), BF16),
        jax.ShapeDtypeStruct((b, s, KV_WIDTH), BF16),
        jax.ShapeDtypeStruct((b, s // kchunk, KV_WIDTH, kchunk), BF16),
        jax.ShapeDtypeStruct((b, s, IDX_WIDTH), BF16),
        jax.ShapeDtypeStruct((b, s, IDX_WIDTH), BF16),
        jax.ShapeDtypeStruct((b, s, 3 * CONV_CHANNELS), BF16),
        jax.ShapeDtypeStruct((b, s, CROSS_WIDTH), BF16),
        jax.ShapeDtypeStruct((b, s, LANES), F32),
    )
    out_specs = (tok(ATT_WIDTH), tok(KV_WIDTH),
                 pl.BlockSpec((1, tm // kchunk, KV_WIDTH, kchunk), lambda si, bi: (bi, si, 0, 0)),
                 tok(IDX_WIDTH), tok(IDX_WIDTH), tok(3 * CONV_CHANNELS), tok(CROSS_WIDTH), tok(LANES))
    return pl.pallas_call(
        functools.partial(_in_proj_kernel, kchunk=kchunk),
        grid=(s // tm, b),
        in_specs=[tok(d), const((1, d)), const((d, _W1_WIDTH))] + [tab] * 6,
        out_specs=out_specs,
        out_shape=out_shape,
        compiler_params=pltpu.CompilerParams(
            dimension_semantics=("arbitrary", "arbitrary"), vmem_limit_bytes=VMEM_LIMIT),
        name="in_proj",
    )(x, g_mix, w1, *tabs)


def _mem_kv_kernel(mem_ref, g_ref, w_ref, km_ref, vm_ref):
    h = _rms(mem_ref[0], g_ref[...]).astype(BF16)
    kv = _dot(h, w_ref[...])
    km_ref[0] = kv[:, :CROSS_WIDTH].astype(BF16)
    vm_ref[0] = kv[:, CROSS_WIDTH:].astype(BF16)


def _mem_kv(mem, g_mem, w_kv):
    b, m, d = mem.shape
    const = lambda shape: pl.BlockSpec(shape, lambda bi: (0,) * len(shape))
    spec = pl.BlockSpec((1, m, CROSS_WIDTH), lambda bi: (bi, 0, 0))
    return pl.pallas_call(
        _mem_kv_kernel,
        grid=(b,),
        in_specs=[pl.BlockSpec((1, m, d), lambda bi: (bi, 0, 0)), const((1, d)),
                  const((d, 2 * CROSS_WIDTH))],
        out_specs=(spec, spec),
        out_shape=(jax.ShapeDtypeStruct((b, m, CROSS_WIDTH), BF16),) * 2,
        compiler_params=pltpu.CompilerParams(dimension_semantics=("arbitrary",)),
        name="mem_kv",
    )(mem, g_mem, w_kv)


def _key_to_float(key):
    return lax.bitcast_convert_type(key ^ ((key >> 31) & 0x7FFFFFFF), F32)


def _col_reduce(x, op):
    rows, n = x.shape
    r = op(x.reshape(SUBLANES, rows // (SUBLANES * SUBLANES), SUBLANES, n), axis=1)
    return op(op(r, axis=0), axis=0, keepdims=True)


def _for_chunks(n, body, init, slab=False):
    def pair(i, carry):
        if slab:
            return body(2 * i, carry, 2)
        return body(2 * i + 1, body(2 * i, carry))
    carry = lax.fori_loop(0, n // 2, pair, init)
    return lax.cond(n % 2 == 1, lambda cr: body(n - 1, cr), lambda cr: cr, carry)


def _dsa_kernel(qi_ref, wi_ref, q_ref, ki_ref, ks_ref, vt_ref, o_ref, sc_ref, sh_ref, acc_ref,
                *, topk, jbits):
    tq = ch = Q_TILE
    batch = range(qi_ref.shape[0])
    qb = pl.program_id(1)
    nc = qb + 1
    qpos = qb * tq + lax.broadcasted_iota(I32, (1, tq), 1)

    def chunk_pos(c):
        off = pl.multiple_of(c * ch, ch)
        return off, off + lax.broadcasted_iota(I32, (ch, tq), 0)

    head_of_lane = lax.broadcasted_iota(I32, (tq, IDX_WIDTH), 1) // IDX_DIM
    zero_bf = jnp.zeros((), BF16)
    qi_heads = [[jnp.where(head_of_lane == h, qi_ref[bb], zero_bf) for h in range(N_IDX_HEADS)]
                for bb in batch]
    wit = [wi_ref[bb].T for bb in batch]
    wrow = [[wit[bb][h:h + 1, :] for h in range(N_IDX_HEADS)] for bb in batch]

    def scores(bb, c, nchunks=1):
        off = pl.multiple_of(c * ch, ch)
        kc = ki_ref[bb, pl.ds(off, nchunks * ch), :]
        acc = jnp.maximum(_dot_nt(kc, qi_heads[bb][0]), 0.0) * wrow[bb][0]
        for h in range(1, N_IDX_HEADS):
            acc = acc + jnp.maximum(_dot_nt(kc, qi_heads[bb][h]), 0.0) * wrow[bb][h]
        return off, acc

    def put_scores(bb, off, s):
        sc_ref[bb, pl.ds(off, s.shape[0]), :] = s
        hi = lax.bitcast_convert_type(lax.bitcast_convert_type(s, I32) & HI16_MASK, F32)
        sh_ref[bb, pl.ds(off, s.shape[0]), :] = hi.astype(BF16)

    def score_chunks(c, carry, nchunks=1):
        for bb in batch:
            put_scores(bb, *scores(bb, c, nchunks))
        return carry

    _for_chunks(qb, score_chunks, 0, slab=True)
    for bb in batch:
        off, acc = scores(bb, qb)
        put_scores(bb, off, jnp.where(chunk_pos(qb)[1] <= qpos, acc, -jnp.inf))

    def count_ge(ref, cands):
        rows = SUBLANES * (4 // ref.dtype.itemsize)
        one, zero = jnp.ones((), ref.dtype), jnp.zeros((), ref.dtype)

        def body(c, cnts):
            off, _ = chunk_pos(c)
            out = []
            for bb in batch:
                hit = jnp.where(ref[bb, pl.ds(off, ch), :] >= cands[bb], one, zero)
                parts = [hit[r:r + rows] for r in range(0, ch, rows)]
                while len(parts) > 1:
                    parts = [a + b for a, b in zip(parts[::2], parts[1::2])]
                out.append(cnts[bb] + parts[0].astype(F32))
            return tuple(out)

        cnts = _for_chunks(nc, body, tuple(jnp.zeros((rows, tq), F32) for _ in batch))
        return [cnt.sum(axis=0, keepdims=True) for cnt in cnts]

    def key_bit(i, carry, ref, to_cand):
        keys, n_keys = carry
        bit = lax.shift_left(jnp.int32(1), 31 - i)
        trials = [key + bit for key in keys]
        n_trials = count_ge(ref, [to_cand(trial) for trial in trials])
        keeps = [n_trial >= topk for n_trial in n_trials]
        return (tuple(jnp.where(keeps[bb], trials[bb], keys[bb]) for bb in batch),
                tuple(jnp.where(keeps[bb], n_trials[bb], n_keys[bb]) for bb in batch))

    def hi_cand(trial):
        pattern = trial ^ ((trial >> 31) & 0x7FFF0000)
        return lax.bitcast_convert_type(pattern, F32).astype(BF16)

    def count(pred):
        def body(c, cnts):
            off, kpos = chunk_pos(c)
            hits = [jnp.where(pred(bb, sc_ref[bb, pl.ds(off, ch), :], kpos), 1.0, 0.0) for bb in batch]
            return tuple(cnts[bb] + hits[bb].reshape(ch // SUBLANES, SUBLANES, tq).sum(axis=0)
                         for bb in batch)
        cnts = _for_chunks(nc, body, tuple(jnp.zeros((SUBLANES, tq), F32) for _ in batch))
        return [cnt.sum(axis=0, keepdims=True) for cnt in cnts]

    def search(carry):
        carry = lax.fori_loop(0, 16, lambda i, cr: key_bit(i, cr, sh_ref, hi_cand), carry)
        return lax.fori_loop(16, 32, lambda i, cr: key_bit(i, cr, sc_ref, _key_to_float), carry)

    init = (tuple(jnp.full((1, tq), INT_MIN, I32) for _ in batch),
            tuple(jnp.zeros((1, tq), F32) for _ in batch))
    keys, n_keys = lax.cond(nc * tq <= topk, lambda cr: cr, search, init)
    taus = [_key_to_float(jnp.maximum(key, KEY_MIN_FINITE)) for key in keys]

    excess = sum(jnp.where(n_key > topk, 1, 0) for n_key in n_keys)

    @pl.when(jnp.max(excess) > 0)
    def _():
        n_above = count(lambda bb, s, _: s > taus[bb])
        wants = [topk - n for n in n_above]

        def pos_bit(i, jmaxs):
            bit = lax.shift_left(jnp.int32(1), jbits - 1 - i)
            trials = [jmax + bit for jmax in jmaxs]
            n_before = count(lambda bb, s, kpos: jnp.where(s == taus[bb], kpos, trials[bb]) < trials[bb])
            return tuple(jnp.where(n_before[bb] <= wants[bb], trials[bb], jmaxs[bb]) for bb in batch)

        jmaxs = lax.fori_loop(0, jbits, pos_bit, tuple(jnp.zeros((1, tq), I32) for _ in batch))

        def drop(c, carry):
            off, kpos = chunk_pos(c)
            for bb in batch:
                s = sc_ref[bb, pl.ds(off, ch), :]
                late_tie = jnp.where(s == taus[bb], kpos, -1) >= jmaxs[bb]
                sc_ref[bb, pl.ds(off, ch), :] = jnp.where(late_tie, -jnp.inf, s)
            return carry

        lax.fori_loop(0, nc, drop, 0)

    group_of_lane = lax.broadcasted_iota(I32, (tq, LANES), 1) // HEAD_DIM
    qe = []
    for bb in batch:
        q = q_ref[bb]
        qe.append([])
        for h in range(N_ATT_HEADS):
            g = h // HEADS_PER_GROUP
            t = q[:, (h // 2) * LANES:(h // 2 + 1) * LANES].astype(F32)
            if h % 2 != g:
                t = pltpu.roll(t, HEAD_DIM, 1)
            qe[bb].append(jnp.where(group_of_lane == g, t, 0.0).astype(BF16))

    acc_ref[...] = jnp.zeros(acc_ref.shape, F32)
    streams = [(bb, h) for bb in batch for h in range(N_ATT_HEADS)]

    def attend(c, carry, nchunks=1):
        ms, ls = carry
        off = pl.multiple_of(c * ch, ch)
        rows = nchunks * ch
        kcs = [ks_ref[bb, pl.ds(off, rows), :] for bb in batch]
        lgs = [_dot_nt(kcs[bb], qe[bb][h]) for bb, h in streams]
        vts, biases = [], []
        for bb in batch:
            vt = vt_ref[bb, c]
            if nchunks == 2:
                vt = jnp.concatenate([vt, vt_ref[bb, c + 1]], axis=1)
            vts.append(vt)
            biases.append(jnp.where(sc_ref[bb, pl.ds(off, rows), :] >= taus[bb], 0.0, NEG_BIG))
        new_ms, new_ls = [], []
        for i, (bb, h) in enumerate(streams):
            lg = biases[bb] + lgs[i]
            m_new = jnp.maximum(ms[i], _col_reduce(lg, jnp.max))
            alpha = jnp.exp2(ms[i] - m_new)
            p = jnp.exp2(lg - m_new)
            new_ms.append(m_new)
            new_ls.append(alpha * ls[i] + _col_reduce(p, jnp.sum))
            cols = slice(h * tq, (h + 1) * tq)
            acc_ref[bb, :, cols] = alpha * acc_ref[bb, :, cols] + _dot(vts[bb], p.astype(BF16))
        return tuple(new_ms), tuple(new_ls)

    init = (tuple(jnp.full((1, tq), NEG_BIG, F32) for _ in streams),
            tuple(jnp.zeros((1, tq), F32) for _ in streams))
    _, ls = _for_chunks(nc, attend, init, slab=True)

    lane = lax.broadcasted_iota(I32, (tq, LANES), 1)
    for bb in batch:
        tiles = []
        for j in range(N_ATT_HEADS // 2):
            parts = []
            for p_ in range(2):
                h = 2 * j + p_
                blk = acc_ref[bb, :, h * tq:(h + 1) * tq] * (1.0 / ls[streams.index((bb, h))])
                if h // HEADS_PER_GROUP != p_:
                    blk = jnp.concatenate([blk[HEAD_DIM:], blk[:HEAD_DIM]], axis=0)
                parts.append(blk.T)
            tiles.append(jnp.where(lane < HEAD_DIM, parts[0], parts[1]))
        o_ref[bb] = jnp.concatenate(tiles, axis=1).astype(BF16)


def _dsa(qi, wi, q, ki, ks, vt):
    b, s, _ = q.shape
    topk = min(TOPK_MAX, s // 4)
    nb = DSA_BATCH if b % DSA_BATCH == 0 else 1
    blk = lambda w: pl.BlockSpec((nb, Q_TILE, w), lambda bi, qb: (bi, qb, 0))
    full = lambda w: pl.BlockSpec((nb, s, w), lambda bi, qb: (bi, 0, 0))
    return pl.pallas_call(
        functools.partial(_dsa_kernel, topk=topk, jbits=int(s).bit_length()),
        grid=(b // nb, s // Q_TILE),
        in_specs=[blk(IDX_WIDTH), blk(LANES), blk(ATT_WIDTH), full(IDX_WIDTH), full(KV_WIDTH),
                  pl.BlockSpec((nb, s // Q_TILE, KV_WIDTH, Q_TILE), lambda bi, qb: (bi, 0, 0, 0))],
        out_specs=blk(ATT_WIDTH),
        out_shape=jax.ShapeDtypeStruct((b, s, ATT_WIDTH), BF16),
        scratch_shapes=[pltpu.VMEM((nb, s, Q_TILE), F32), pltpu.VMEM((nb, s, Q_TILE), BF16),
                        pltpu.VMEM((nb, KV_WIDTH, N_ATT_HEADS * Q_TILE), F32)],
        compiler_params=pltpu.CompilerParams(
            dimension_semantics=("arbitrary", "arbitrary"), vmem_limit_bytes=VMEM_LIMIT),
        name="dsa",
    )(qi, wi, q, ki, ks, vt)


def _causal_conv3(u, w, carry_ref):
    prev = carry_ref[...]
    row = lax.broadcasted_iota(I32, prev.shape, 0)
    s1 = pltpu.roll(u, 1, 0)
    s2 = pltpu.roll(u, 2, 0)
    p1 = pltpu.roll(prev, 1, 0)
    p2 = pltpu.roll(prev, 2, 0)
    s1 = jnp.concatenate([jnp.where(row < 1, p1, s1[:SUBLANES]), s1[SUBLANES:]], axis=0)
    s2 = jnp.concatenate([jnp.where(row < 2, p2, s2[:SUBLANES]), s2[SUBLANES:]], axis=0)
    carry_ref[...] = u[u.shape[0] - SUBLANES:]
    return s2 * w[0:1] + s1 * w[1:2] + u * w[2:3]


def _merge_kernel(x_ref, g_ref, att_ref, conv_ref, qc_ref, km_ref, vm_ref, wg_ref, bg_ref, cw_ref,
                  wa_ref, wc_ref, wm_ref, wo_ref, o_ref, carry_ref):
    @pl.when(pl.program_id(1) == 0)
    def _():
        carry_ref[...] = jnp.zeros(carry_ref.shape, F32)

    x = x_ref[0]
    h = _rms(x, g_ref[...]).astype(BF16)

    y_att = _dot(att_ref[0], wa_ref[...])

    cin = conv_ref[0].astype(F32)
    c = CONV_CHANNELS
    cu = cin[:, c:2 * c] * cin[:, 2 * c:]
    yc = cin[:, :c] * _causal_conv3(cu, cw_ref[...], carry_ref)
    y_conv = _dot(yc.astype(BF16), wc_ref[...])

    qc, km, vm = qc_ref[0], km_ref[0], vm_ref[0]
    head_of_lane = lax.broadcasted_iota(I32, km.shape, 1) // HEAD_DIM
    zero_bf = jnp.zeros((), BF16)
    mem_out = jnp.zeros((x.shape[0], CROSS_WIDTH), F32)
    for hh in range(N_CROSS_HEADS):
        lg = _dot_nt(qc, jnp.where(head_of_lane == hh, km, zero_bf)) * (HEAD_DIM ** -0.5)
        e = jnp.exp(lg - lg.max(axis=-1, keepdims=True))
        p = e / e.sum(axis=-1, keepdims=True)
        mem_out = mem_out + _dot(p.astype(BF16), jnp.where(head_of_lane == hh, vm, zero_bf))
    y_mem = _dot(mem_out.astype(BF16), wm_ref[...])

    d = x.shape[1]
    merged = jnp.zeros(x.shape, F32)
    for i, y in enumerate((y_att, y_conv, y_mem)):
        pre = bg_ref[:, i * d:(i + 1) * d] + _dot(h, wg_ref[:, i * d:(i + 1) * d])
        merged = merged + jax.nn.sigmoid(pre) * y
    o_ref[0] = x + _dot(merged.astype(BF16), wo_ref[...])


def _merge(x, g_mix, att, conv_in, qc, km, vm, wg, b_gate, conv_w, wa, wc, wm, wo, tm):
    b, s, d = x.shape
    m = km.shape[1]
    tok = lambda w: pl.BlockSpec((1, tm, w), lambda bi, si: (bi, si, 0))
    const = lambda shape: pl.BlockSpec(shape, lambda bi, si: (0,) * len(shape),
                                       pipeline_mode=pl.Buffered(1))
    memspec = pl.BlockSpec((1, m, CROSS_WIDTH), lambda bi, si: (bi, 0, 0))
    return pl.pallas_call(
        _merge_kernel,
        grid=(b, s // tm),
        in_specs=[tok(d), const((1, d)), tok(ATT_WIDTH), tok(3 * CONV_CHANNELS), tok(CROSS_WIDTH),
                  memspec, memspec, const((d, N_BRANCHES * d)), const((1, N_BRANCHES * d)),
                  const((CONV_K, CONV_CHANNELS)), const((ATT_WIDTH, d)), const((CONV_CHANNELS, d)),
                  const((CROSS_WIDTH, d)), const((d, d))],
        out_specs=tok(d),
        out_shape=jax.ShapeDtypeStruct((b, s, d), F32),
        scratch_shapes=[pltpu.VMEM((SUBLANES, CONV_CHANNELS), F32)],
        compiler_params=pltpu.CompilerParams(
            dimension_semantics=("arbitrary", "arbitrary"), vmem_limit_bytes=VMEM_LIMIT),
        name="merge",
    )(x, g_mix, att, conv_in, qc, km, vm, wg, b_gate, conv_w, wa, wc, wm, wo)


def _ffn_kernel(x_ref, g_ref, wu_ref, cw_ref, wd_ref, gf_ref, o_ref, carry_ref, act_ref, *, fc):
    @pl.when(pl.program_id(1) == 0)
    def _():
        carry_ref[...] = jnp.zeros(carry_ref.shape, F32)

    x = x_ref[0]
    h = _rms(x, g_ref[...]).astype(BF16)
    dff = wd_ref.shape[0]
    for c0 in range(0, dff, fc):
        halves = []
        for base in (0, dff):
            cols = slice(base + c0, base + c0 + fc)
            up = _dot(h, wu_ref[:, cols])
            halves.append(_causal_conv3(up, cw_ref[:, cols], carry_ref.at[:, cols]))
        gate, val = halves
        act_ref[:, c0:c0 + fc] = (gate * jax.nn.sigmoid(gate) * val).astype(BF16)
    y = x + _dot(act_ref[...], wd_ref[...])
    o_ref[0] = _rms(y, gf_ref[...])


def _ffn(x, g_ffn, wu, conv_w, wd, g_final, tm, fc):
    b, s, d = x.shape
    dff = wd.shape[0]
    tok = pl.BlockSpec((1, tm, d), lambda bi, si: (bi, si, 0))
    const = lambda shape: pl.BlockSpec(shape, lambda bi, si: (0,) * len(shape),
                                       pipeline_mode=pl.Buffered(1))
    return pl.pallas_call(
        functools.partial(_ffn_kernel, fc=fc),
        grid=(b, s // tm),
        in_specs=[tok, const((1, d)), const((d, 2 * dff)), const((CONV_K, 2 * dff)),
                  const((dff, d)), const((1, d))],
        out_specs=tok,
        out_shape=jax.ShapeDtypeStruct((b, s, d), F32),
        scratch_shapes=[pltpu.VMEM((SUBLANES, 2 * dff), F32), pltpu.VMEM((tm, dff), BF16)],
        compiler_params=pltpu.CompilerParams(
            dimension_semantics=("arbitrary", "arbitrary"), vmem_limit_bytes=VMEM_LIMIT),
        name="ffn",
    )(x, g_ffn, wu, conv_w, wd, g_final)


def _pack_in_proj_weight(w):
    sizes = (ATT_WIDTH, KV_WIDTH, KV_WIDTH, IDX_WIDTH, IDX_DIM, N_IDX_HEADS,
             3 * CONV_CHANNELS, CROSS_WIDTH)
    offs = np.concatenate([[0], np.cumsum(sizes)])
    wq, wk, wv, wqi, wki, wwi, wconv, wqc = [w[:, offs[i]:offs[i + 1]] for i in range(len(sizes))]
    idx_scale = (N_IDX_HEADS * IDX_DIM) ** -0.5
    att_scale = HEAD_DIM ** -0.5
    wwi = jnp.pad(wwi * idx_scale, ((0, 0), (0, LANES - N_IDX_HEADS)))
    w1 = jnp.concatenate([wq * att_scale, wwi, wk, wv, wqi, jnp.tile(wki, (1, N_IDX_HEADS)),
                          wconv, wqc], axis=1)
    return w1.astype(BF16), w[:, offs[-1]:].astype(BF16)


def _layer(x, mem, g_mix, w_in, b_gate, conv_w_short, w_att_out, w_conv_out, w_mem_out, w_o,
           g_mem, w_mem_kv, g_ffn, w_up, conv_w_ffn, w_down, g_final):
    b, s, d = x.shape
    tm, tm_ffn = min(TOKEN_TILE, s), min(FFN_TILE, s)
    assert s % tm == 0 and s % tm_ffn == 0 and tm % Q_TILE == 0 and d % LANES == 0, (s, d)
    w1, wg = _pack_in_proj_weight(w_in)
    q, ks, vt, qi, ki, conv_in, qc, wi = _in_proj(x, g_mix[None], w1, tm, Q_TILE)
    km, vm = _mem_kv(mem, g_mem[None], w_mem_kv.astype(BF16))
    att = _dsa(qi, wi, q, ki, ks, vt)
    x1 = _merge(x, g_mix[None], att, conv_in, qc, km, vm, wg, b_gate[None], conv_w_short,
                w_att_out.astype(BF16), w_conv_out.astype(BF16), w_mem_out.astype(BF16),
                w_o.astype(BF16), tm)
    dff = w_down.shape[0]
    fc = FFN_CHUNK if dff % FFN_CHUNK == 0 else dff
    return _ffn(x1, g_ffn[None], w_up.astype(BF16), conv_w_ffn, w_down.astype(BF16),
                g_final[None], tm_ffn, fc)


def kernel(x, mem, g_mix, w_in, b_gate, conv_w_short, w_att_out, w_conv_out, w_mem_out, w_o,
           g_mem, w_mem_kv, g_ffn, w_up, conv_w_ffn, w_down, g_final):
    depth = w_in.shape[0]
    assert depth == 1, "the final RMSNorm is fused into the last layer's ffn kernel"
    return _layer(x, mem, g_mix[0], w_in[0], b_gate[0], conv_w_short[0], w_att_out[0],
                  w_conv_out[0], w_mem_out[0], w_o[0], g_mem[0], w_mem_kv[0], g_ffn[0],
                  w_up[0], conv_w_ffn[0], w_down[0], g_final)
```

```python
import functools

import numpy as np
import jax
import jax.numpy as jnp
from jax import lax
from jax.experimental import pallas as pl
from jax.experimental.pallas import tpu as pltpu

F32 = jnp.float32
BF16 = jnp.bfloat16
I32 = jnp.int32

HEAD_DIM = 64
N_ATT_HEADS = 6
N_KV_GROUPS = 2
HEADS_PER_GROUP = N_ATT_HEADS // N_KV_GROUPS
ATT_WIDTH = N_ATT_HEADS * HEAD_DIM
KV_WIDTH = N_KV_GROUPS * HEAD_DIM
N_IDX_HEADS = 8
IDX_DIM = 32
IDX_WIDTH = N_IDX_HEADS * IDX_DIM
TOPK_MAX = 256
CONV_CHANNELS = 384
CONV_K = 3
N_CROSS_HEADS = 4
CROSS_WIDTH = N_CROSS_HEADS * HEAD_DIM
N_BRANCHES = 3
ROPE_THETA = 500000.0
ROPE_FRACTION = 4
NORM_EPS = 1e-6

LANES = 128
SUBLANES = 8
VMEM_LIMIT = 56 * 1024 * 1024

TOKEN_TILE = 1024
FFN_TILE = 1024
FFN_CHUNK = 256
Q_TILE = 256
DSA_BATCH = 4
NEG_BIG = -1e30
LOG2_E = 1.4426950408889634
INT_MIN = -2 ** 31
KEY_MIN_FINITE = -2139095040
HI16_MASK = -65536


def _rms(x, g):
    return x * lax.rsqrt(jnp.mean(x * x, axis=-1, keepdims=True) + NORM_EPS) * g


def _dot_nt(a, b):
    return lax.dot_general(a, b, (((1,), (1,)), ((), ())), preferred_element_type=F32)


def _dot(a, b):
    return jnp.dot(a, b, preferred_element_type=F32)


def _rope_tile(x, c, sa, sb, half):
    return x * c + pltpu.roll(x, LANES - half, 1) * sa + pltpu.roll(x, half, 1) * sb


def _rope(x, c, sa, sb, half):
    tiles = [_rope_tile(x[:, i:i + LANES], c, sa, sb, half) for i in range(0, x.shape[1], LANES)]
    return tiles[0] if len(tiles) == 1 else jnp.concatenate(tiles, axis=1)


_OFF_Q = 0
_OFF_WI = _OFF_Q + ATT_WIDTH
_OFF_KV = _OFF_WI + LANES
_OFF_QI = _OFF_KV + 2 * KV_WIDTH
_OFF_KI = _OFF_QI + IDX_WIDTH
_OFF_CONV = _OFF_KI + IDX_WIDTH
_OFF_QC = _OFF_CONV + 3 * CONV_CHANNELS
_W1_WIDTH = _OFF_QC + CROSS_WIDTH


def _in_proj_kernel(x_ref, g_ref, w_ref, ca_ref, saa_ref, sba_ref, ci_ref, sai_ref, sbi_ref,
                    q_ref, ks_ref, vt_ref, qi_ref, ki_ref, conv_ref, qc_ref, wi_ref, *, kchunk):
    h = _rms(x_ref[0], g_ref[...]).astype(BF16)
    ca, saa, sba = ca_ref[...], saa_ref[...], sba_ref[...]
    ci, sai, sbi = ci_ref[...], sai_ref[...], sbi_ref[...]
    half_a = HEAD_DIM // ROPE_FRACTION // 2
    half_i = IDX_DIM // ROPE_FRACTION // 2

    a = _dot(h, w_ref[:, _OFF_Q:_OFF_KV])
    q_ref[0] = (_rope(a[:, :ATT_WIDTH], ca, saa, sba, half_a) * LOG2_E).astype(BF16)
    wi_ref[0] = a[:, ATT_WIDTH:]

    kv = _dot(h, w_ref[:, _OFF_KV:_OFF_QI])
    ks_ref[0] = _rope(kv[:, :KV_WIDTH], ca, saa, sba, half_a).astype(BF16)
    v = kv[:, KV_WIDTH:]
    for i in range(v.shape[0] // kchunk):
        vt_ref[0, i] = v[i * kchunk:(i + 1) * kchunk].T.astype(BF16)

    idx = _dot(h, w_ref[:, _OFF_QI:_OFF_CONV])
    qi_ref[0] = _rope(idx[:, :IDX_WIDTH], ci, sai, sbi, half_i).astype(BF16)
    ki_ref[0] = _rope(idx[:, IDX_WIDTH:], ci, sai, sbi, half_i).astype(BF16)

    cq = _dot(h, w_ref[:, _OFF_CONV:_W1_WIDTH])
    conv_ref[0] = cq[:, :3 * CONV_CHANNELS].astype(BF16)
    qc_ref[0] = cq[:, 3 * CONV_CHANNELS:].astype(BF16)


def _rope_tables(s, period, rot_dim):
    half = rot_dim // 2
    inv_freq = ROPE_THETA ** (-jnp.arange(half, dtype=F32) / half)
    ang = jnp.arange(s).astype(F32)[:, None] * inv_freq[None, :]
    cos, sin = jnp.cos(ang), jnp.sin(ang)
    j = np.arange(LANES) % period
    first = jnp.asarray(j < half)[None, :]
    second = jnp.asarray((j >= half) & (j < rot_dim))[None, :]
    fidx = np.where(j < half, j, np.where(j < rot_dim, j - half, 0))
    cos_l, sin_l = cos[:, fidx], sin[:, fidx]
    c = jnp.where(first | second, cos_l, 1.0)
    sa = jnp.where(first, -sin_l, 0.0)
    sb = jnp.where(second, sin_l, 0.0)
    return c, sa, sb


def _in_proj(x, g_mix, w1, tm, kchunk):
    b, s, d = x.shape
    tabs = _rope_tables(s, HEAD_DIM, HEAD_DIM // ROPE_FRACTION) + \
        _rope_tables(s, IDX_DIM, IDX_DIM // ROPE_FRACTION)
    tok = lambda w: pl.BlockSpec((1, tm, w), lambda si, bi: (bi, si, 0))
    tab = pl.BlockSpec((tm, LANES), lambda si, bi: (si, 0))
    const = lambda shape: pl.BlockSpec(shape, lambda si, bi: (0,) * len(shape),
                                       pipeline_mode=pl.Buffered(1))
    out_shape = (
        jax.ShapeDtypeStruct((b, s, ATT_WIDTH), BF16),
        jax.ShapeDtypeStruct((b, s, KV_WIDTH), BF16),
        jax.ShapeDtypeStruct((b, s // kchunk, KV_WIDTH, kchunk), BF16),
        jax.ShapeDtypeStruct((b, s, IDX_WIDTH), BF16),
        jax.ShapeDtypeStruct((b, s, IDX_WIDTH), BF16),
        jax.ShapeDtypeStruct((b, s, 3 * CONV_CHANNELS), BF16),
        jax.ShapeDtypeStruct((b, s, CROSS_WIDTH), BF16),
        jax.ShapeDtypeStruct((b, s, LANES), F32),
    )
    out_specs = (tok(ATT_WIDTH), tok(KV_WIDTH),
                 pl.BlockSpec((1, tm // kchunk, KV_WIDTH, kchunk), lambda si, bi: (bi, si, 0, 0)),
                 tok(IDX_WIDTH), tok(IDX_WIDTH), tok(3 * CONV_CHANNELS), tok(CROSS_WIDTH), tok(LANES))
    return pl.pallas_call(
        functools.partial(_in_proj_kernel, kchunk=kchunk),
        grid=(s // tm, b),
        in_specs=[tok(d), const((1, d)), const((d, _W1_WIDTH))] + [tab] * 6,
        out_specs=out_specs,
        out_shape=out_shape,
        compiler_params=pltpu.CompilerParams(
            dimension_semantics=("arbitrary", "arbitrary"), vmem_limit_bytes=VMEM_LIMIT),
        name="in_proj",
    )(x, g_mix, w1, *tabs)


def _mem_kv_kernel(mem_ref, g_ref, w_ref, km_ref, vm_ref):
    h = _rms(mem_ref[0], g_ref[...]).astype(BF16)
    kv = _dot(h, w_ref[...])
    km_ref[0] = kv[:, :CROSS_WIDTH].astype(BF16)
    vm_ref[0] = kv[:, CROSS_WIDTH:].astype(BF16)


def _mem_kv(mem, g_mem, w_kv):
    b, m, d = mem.shape
    const = lambda shape: pl.BlockSpec(shape, lambda bi: (0,) * len(shape))
    spec = pl.BlockSpec((1, m, CROSS_WIDTH), lambda bi: (bi, 0, 0))
    return pl.pallas_call(
        _mem_kv_kernel,
        grid=(b,),
        in_specs=[pl.BlockSpec((1, m, d), lambda bi: (bi, 0, 0)), const((1, d)),
                  const((d, 2 * CROSS_WIDTH))],
        out_specs=(spec, spec),
        out_shape=(jax.ShapeDtypeStruct((b, m, CROSS_WIDTH), BF16),) * 2,
        compiler_params=pltpu.CompilerParams(dimension_semantics=("arbitrary",)),
        name="mem_kv",
    )(mem, g_mem, w_kv)


def _key_to_float(key):
    return lax.bitcast_convert_type(key ^ ((key >> 31) & 0x7FFFFFFF), F32)


def _col_reduce(x, op):
    rows, n = x.shape
    r = op(x.reshape(SUBLANES, rows // (SUBLANES * SUBLANES), SUBLANES, n), axis=1)
    return op(op(r, axis=0), axis=0, keepdims=True)


def _for_chunks(n, body, init, slab=False):
    def pair(i, carry):
        if slab:
            return body(2 * i, carry, 2)
        return body(2 * i + 1, body(2 * i, carry))
    carry = lax.fori_loop(0, n // 2, pair, init)
    return lax.cond(n % 2 == 1, lambda cr: body(n - 1, cr), lambda cr: cr, carry)


def _dsa_kernel(qi_ref, wi_ref, q_ref, ki_ref, ks_ref, vt_ref, o_ref, sc_ref, sh_ref, acc_ref,
                *, topk, jbits):
    tq = ch = Q_TILE
    batch = range(qi_ref.shape[0])
    qb = pl.program_id(1)
    nc = qb + 1
    qpos = qb * tq + lax.broadcasted_iota(I32, (1, tq), 1)

    def chunk_pos(c):
        off = pl.multiple_of(c * ch, ch)
        return off, off + lax.broadcasted_iota(I32, (ch, tq), 0)

    head_of_lane = lax.broadcasted_iota(I32, (tq, IDX_WIDTH), 1) // IDX_DIM
    zero_bf = jnp.zeros((), BF16)
    qi_heads = [[jnp.where(head_of_lane == h, qi_ref[bb], zero_bf) for h in range(N_IDX_HEADS)]
                for bb in batch]
    wit = [wi_ref[bb].T for bb in batch]
    wrow = [[wit[bb][h:h + 1, :] for h in range(N_IDX_HEADS)] for bb in batch]

    def scores(bb, c, nchunks=1):
        off = pl.multiple_of(c * ch, ch)
        kc = ki_ref[bb, pl.ds(off, nchunks * ch), :]
        acc = jnp.maximum(_dot_nt(kc, qi_heads[bb][0]), 0.0) * wrow[bb][0]
        for h in range(1, N_IDX_HEADS):
            acc = acc + jnp.maximum(_dot_nt(kc, qi_heads[bb][h]), 0.0) * wrow[bb][h]
        return off, acc

    def put_scores(bb, off, s):
        sc_ref[bb, pl.ds(off, s.shape[0]), :] = s
        hi = lax.bitcast_convert_type(lax.bitcast_convert_type(s, I32) & HI16_MASK, F32)
        sh_ref[bb, pl.ds(off, s.shape[0]), :] = hi.astype(BF16)

    def score_chunks(c, carry, nchunks=1):
        kpos = c * ch + lax.broadcasted_iota(I32, (nchunks * ch, tq), 0)
        for bb in batch:
            off, acc = scores(bb, c, nchunks)
            put_scores(bb, off, jnp.where(kpos <= qpos, acc, -jnp.inf))
        return carry

    _for_chunks(nc, score_chunks, 0, slab=True)

    def count_ge(ref, cands):
        rows = SUBLANES * (4 // ref.dtype.itemsize)
        one, zero = jnp.ones((), ref.dtype), jnp.zeros((), ref.dtype)

        def body(c, cnts):
            off, _ = chunk_pos(c)
            out = []
            for bb in batch:
                hit = jnp.where(ref[bb, pl.ds(off, ch), :] >= cands[bb], one, zero)
                parts = [hit[r:r + rows] for r in range(0, ch, rows)]
                while len(parts) > 1:
                    parts = [a + b for a, b in zip(parts[::2], parts[1::2])]
                out.append(cnts[bb] + parts[0].astype(F32))
            return tuple(out)

        cnts = _for_chunks(nc, body, tuple(jnp.zeros((rows, tq), F32) for _ in batch))
        return [cnt.sum(axis=0, keepdims=True) for cnt in cnts]

    def key_bit(i, carry, ref, to_cand):
        keys, n_keys = carry
        bit = lax.shift_left(jnp.int32(1), 31 - i)
        trials = [key + bit for key in keys]
        n_trials = count_ge(ref, [to_cand(trial) for trial in trials])
        keeps = [n_trial >= topk for n_trial in n_trials]
        return (tuple(jnp.where(keeps[bb], trials[bb], keys[bb]) for bb in batch),
                tuple(jnp.where(keeps[bb], n_trials[bb], n_keys[bb]) for bb in batch))

    def hi_cand(trial):
        pattern = trial ^ ((trial >> 31) & 0x7FFF0000)
        return lax.bitcast_convert_type(pattern, F32).astype(BF16)

    def count(pred):
        def body(c, cnts):
            off, kpos = chunk_pos(c)
            hits = [jnp.where(pred(bb, sc_ref[bb, pl.ds(off, ch), :], kpos), 1.0, 0.0) for bb in batch]
            return tuple(cnts[bb] + hits[bb].reshape(ch // SUBLANES, SUBLANES, tq).sum(axis=0)
                         for bb in batch)
        cnts = _for_chunks(nc, body, tuple(jnp.zeros((SUBLANES, tq), F32) for _ in batch))
        return [cnt.sum(axis=0, keepdims=True) for cnt in cnts]

    def search(carry):
        carry = lax.fori_loop(0, 16, lambda i, cr: key_bit(i, cr, sh_ref, hi_cand), carry)
        return lax.fori_loop(16, 32, lambda i, cr: key_bit(i, cr, sc_ref, _key_to_float), carry)

    init = (tuple(jnp.full((1, tq), INT_MIN, I32) for _ in batch),
            tuple(jnp.zeros((1, tq), F32) for _ in batch))
    keys, n_keys = lax.cond(nc * tq <= topk, lambda cr: cr, search, init)
    taus = [_key_to_float(jnp.maximum(key, KEY_MIN_FINITE)) for key in keys]

    excess = sum(jnp.where(n_key > topk, 1, 0) for n_key in n_keys)

    @pl.when(jnp.max(excess) > 0)
    def _():
        n_above = count(lambda bb, s, _: s > taus[bb])
        wants = [topk - n for n in n_above]

        def pos_bit(i, jmaxs):
            bit = lax.shift_left(jnp.int32(1), jbits - 1 - i)
            trials = [jmax + bit for jmax in jmaxs]
            n_before = count(lambda bb, s, kpos: jnp.where(s == taus[bb], kpos, trials[bb]) < trials[bb])
            return tuple(jnp.where(n_before[bb] <= wants[bb], trials[bb], jmaxs[bb]) for bb in batch)

        jmaxs = lax.fori_loop(0, jbits, pos_bit, tuple(jnp.zeros((1, tq), I32) for _ in batch))

        def drop(c, carry):
            off, kpos = chunk_pos(c)
            for bb in batch:
                s = sc_ref[bb, pl.ds(off, ch), :]
                late_tie = jnp.where(s == taus[bb], kpos, -1) >= jmaxs[bb]
                sc_ref[bb, pl.ds(off, ch), :] = jnp.where(late_tie, -jnp.inf, s)
            return carry

        lax.fori_loop(0, nc, drop, 0)

    group_of_lane = lax.broadcasted_iota(I32, (tq, LANES), 1) // HEAD_DIM
    qe = []
    for bb in batch:
        q = q_ref[bb]
        qe.append([])
        for h in range(N_ATT_HEADS):
            g = h // HEADS_PER_GROUP
            t = q[:, (h // 2) * LANES:(h // 2 + 1) * LANES].astype(F32)
            if h % 2 != g:
                t = pltpu.roll(t, HEAD_DIM, 1)
            qe[bb].append(jnp.where(group_of_lane == g, t, 0.0).astype(BF16))

    acc_ref[...] = jnp.zeros(acc_ref.shape, F32)
    streams = [(bb, h) for bb in batch for h in range(N_ATT_HEADS)]

    def attend(c, carry, nchunks=1):
        ms, ls = carry
        off = pl.multiple_of(c * ch, ch)
        rows = nchunks * ch
        kcs = [ks_ref[bb, pl.ds(off, rows), :] for bb in batch]
        lgs = [_dot_nt(kcs[bb], qe[bb][h]) for bb, h in streams]
        vts, biases = [], []
        for bb in batch:
            vt = vt_ref[bb, c]
            if nchunks == 2:
                vt = jnp.concatenate([vt, vt_ref[bb, c + 1]], axis=1)
            vts.append(vt)
            biases.append(jnp.where(sc_ref[bb, pl.ds(off, rows), :] >= taus[bb], 0.0, NEG_BIG))
        new_ms, new_ls = [], []
        for i, (bb, h) in enumerate(streams):
            lg = biases[bb] + lgs[i]
            m_new = jnp.maximum(ms[i], _col_reduce(lg, jnp.max))
            alpha = jnp.exp2(ms[i] - m_new)
            p = jnp.exp2(lg - m_new)
            new_ms.append(m_new)
            new_ls.append(alpha * ls[i] + _col_reduce(p, jnp.sum))
            cols = slice(h * tq, (h + 1) * tq)
            acc_ref[bb, :, cols] = alpha * acc_ref[bb, :, cols] + _dot(vts[bb], p.astype(BF16))
        return tuple(new_ms), tuple(new_ls)

    init = (tuple(jnp.full((1, tq), NEG_BIG, F32) for _ in streams),
            tuple(jnp.zeros((1, tq), F32) for _ in streams))
    _, ls = _for_chunks(nc, attend, init, slab=True)

    lane = lax.broadcasted_iota(I32, (tq, LANES), 1)
    for bb in batch:
        tiles = []
        for j in range(N_ATT_HEADS // 2):
            parts = []
            for p_ in range(2):
                h = 2 * j + p_
                blk = acc_ref[bb, :, h * tq:(h + 1) * tq] * (1.0 / ls[streams.index((bb, h))])
                if h // HEADS_PER_GROUP != p_:
                    blk = jnp.concatenate([blk[HEAD_DIM:], blk[:HEAD_DIM]], axis=0)
                parts.append(blk.T)
            tiles.append(jnp.where(lane < HEAD_DIM, parts[0], parts[1]))
        o_ref[bb] = jnp.concatenate(tiles, axis=1).astype(BF16)


def _dsa(qi, wi, q, ki, ks, vt):
    b, s, _ = q.shape
    topk = min(TOPK_MAX, s // 4)
    nb = DSA_BATCH if b % DSA_BATCH == 0 else 1
    blk = lambda w: pl.BlockSpec((nb, Q_TILE, w), lambda bi, qb: (bi, qb, 0))
    full = lambda w: pl.BlockSpec((nb, s, w), lambda bi, qb: (bi, 0, 0))
    return pl.pallas_call(
        functools.partial(_dsa_kernel, topk=topk, jbits=int(s).bit_length()),
        grid=(b // nb, s // Q_TILE),
        in_specs=[blk(IDX_WIDTH), blk(LANES), blk(ATT_WIDTH), full(IDX_WIDTH), full(KV_WIDTH),
                  pl.BlockSpec((nb, s // Q_TILE, KV_WIDTH, Q_TILE), lambda bi, qb: (bi, 0, 0, 0))],
        out_specs=blk(ATT_WIDTH),
        out_shape=jax.ShapeDtypeStruct((b, s, ATT_WIDTH), BF16),
        scratch_shapes=[pltpu.VMEM((nb, s, Q_TILE), F32), pltpu.VMEM((nb, s, Q_TILE), BF16),
                        pltpu.VMEM((nb, KV_WIDTH, N_ATT_HEADS * Q_TILE), F32)],
        compiler_params=pltpu.CompilerParams(
            dimension_semantics=("arbitrary", "arbitrary"), vmem_limit_bytes=VMEM_LIMIT),
        name="dsa",
    )(qi, wi, q, ki, ks, vt)


def _causal_conv3(u, w, carry_ref):
    prev = carry_ref[...]
    row = lax.broadcasted_iota(I32, prev.shape, 0)
    s1 = pltpu.roll(u, 1, 0)
    s2 = pltpu.roll(u, 2, 0)
    p1 = pltpu.roll(prev, 1, 0)
    p2 = pltpu.roll(prev, 2, 0)
    s1 = jnp.concatenate([jnp.where(row < 1, p1, s1[:SUBLANES]), s1[SUBLANES:]], axis=0)
    s2 = jnp.concatenate([jnp.where(row < 2, p2, s2[:SUBLANES]), s2[SUBLANES:]], axis=0)
    carry_ref[...] = u[u.shape[0] - SUBLANES:]
    return s2 * w[0:1] + s1 * w[1:2] + u * w[2:3]


def _merge_kernel(x_ref, g_ref, att_ref, conv_ref, qc_ref, km_ref, vm_ref, wg_ref, bg_ref, cw_ref,
                  wa_ref, wc_ref, wm_ref, wo_ref, o_ref, carry_ref):
    @pl.when(pl.program_id(1) == 0)
    def _():
        carry_ref[...] = jnp.zeros(carry_ref.shape, F32)

    x = x_ref[0]
    h = _rms(x, g_ref[...]).astype(BF16)

    y_att = _dot(att_ref[0], wa_ref[...])

    cin = conv_ref[0].astype(F32)
    c = CONV_CHANNELS
    cu = cin[:, c:2 * c] * cin[:, 2 * c:]
    yc = cin[:, :c] * _causal_conv3(cu, cw_ref[...], carry_ref)
    y_conv = _dot(yc.astype(BF16), wc_ref[...])

    qc, km, vm = qc_ref[0], km_ref[0], vm_ref[0]
    head_of_lane = lax.broadcasted_iota(I32, km.shape, 1) // HEAD_DIM
    zero_bf = jnp.zeros((), BF16)
    mem_out = jnp.zeros((x.shape[0], CROSS_WIDTH), F32)
    for hh in range(N_CROSS_HEADS):
        lg = _dot_nt(qc, jnp.where(head_of_lane == hh, km, zero_bf)) * (HEAD_DIM ** -0.5)
        e = jnp.exp(lg - lg.max(axis=-1, keepdims=True))
        p = e / e.sum(axis=-1, keepdims=True)
        mem_out = mem_out + _dot(p.astype(BF16), jnp.where(head_of_lane == hh, vm, zero_bf))
    y_mem = _dot(mem_out.astype(BF16), wm_ref[...])

    d = x.shape[1]
    merged = jnp.zeros(x.shape, F32)
    for i, y in enumerate((y_att, y_conv, y_mem)):
        pre = bg_ref[:, i * d:(i + 1) * d] + _dot(h, wg_ref[:, i * d:(i + 1) * d])
        merged = merged + jax.nn.sigmoid(pre) * y
    o_ref[0] = x + _dot(merged.astype(BF16), wo_ref[...])


def _merge(x, g_mix, att, conv_in, qc, km, vm, wg, b_gate, conv_w, wa, wc, wm, wo, tm):
    b, s, d = x.shape
    m = km.shape[1]
    tok = lambda w: pl.BlockSpec((1, tm, w), lambda bi, si: (bi, si, 0))
    const = lambda shape: pl.BlockSpec(shape, lambda bi, si: (0,) * len(shape),
                                       pipeline_mode=pl.Buffered(1))
    memspec = pl.BlockSpec((1, m, CROSS_WIDTH), lambda bi, si: (bi, 0, 0))
    return pl.pallas_call(
        _merge_kernel,
        grid=(b, s // tm),
        in_specs=[tok(d), const((1, d)), tok(ATT_WIDTH), tok(3 * CONV_CHANNELS), tok(CROSS_WIDTH),
                  memspec, memspec, const((d, N_BRANCHES * d)), const((1, N_BRANCHES * d)),
                  const((CONV_K, CONV_CHANNELS)), const((ATT_WIDTH, d)), const((CONV_CHANNELS, d)),
                  const((CROSS_WIDTH, d)), const((d, d))],
        out_specs=tok(d),
        out_shape=jax.ShapeDtypeStruct((b, s, d), F32),
        scratch_shapes=[pltpu.VMEM((SUBLANES, CONV_CHANNELS), F32)],
        compiler_params=pltpu.CompilerParams(
            dimension_semantics=("arbitrary", "arbitrary"), vmem_limit_bytes=VMEM_LIMIT),
        name="merge",
    )(x, g_mix, att, conv_in, qc, km, vm, wg, b_gate, conv_w, wa, wc, wm, wo)


def _ffn_kernel(x_ref, g_ref, wu_ref, cw_ref, wd_ref, gf_ref, o_ref, carry_ref, act_ref, *, fc):
    @pl.when(pl.program_id(1) == 0)
    def _():
        carry_ref[...] = jnp.zeros(carry_ref.shape, F32)

    x = x_ref[0]
    h = _rms(x, g_ref[...]).astype(BF16)
    dff = wd_ref.shape[0]
    for c0 in range(0, dff, fc):
        halves = []
        for base in (0, dff):
            cols = slice(base + c0, base + c0 + fc)
            up = _dot(h, wu_ref[:, cols])
            halves.append(_causal_conv3(up, cw_ref[:, cols], carry_ref.at[:, cols]))
        gate, val = halves
        act_ref[:, c0:c0 + fc] = (gate * jax.nn.sigmoid(gate) * val).astype(BF16)
    y = x + _dot(act_ref[...], wd_ref[...])
    o_ref[0] = _rms(y, gf_ref[...])


def _ffn(x, g_ffn, wu, conv_w, wd, g_final, tm, fc):
    b, s, d = x.shape
    dff = wd.shape[0]
    tok = pl.BlockSpec((1, tm, d), lambda bi, si: (bi, si, 0))
    const = lambda shape: pl.BlockSpec(shape, lambda bi, si: (0,) * len(shape),
                                       pipeline_mode=pl.Buffered(1))
    return pl.pallas_call(
        functools.partial(_ffn_kernel, fc=fc),
        grid=(b, s // tm),
        in_specs=[tok, const((1, d)), const((d, 2 * dff)), const((CONV_K, 2 * dff)),
                  const((dff, d)), const((1, d))],
        out_specs=tok,
        out_shape=jax.ShapeDtypeStruct((b, s, d), F32),
        scratch_shapes=[pltpu.VMEM((SUBLANES, 2 * dff), F32), pltpu.VMEM((tm, dff), BF16)],
        compiler_params=pltpu.CompilerParams(
            dimension_semantics=("arbitrary", "arbitrary"), vmem_limit_bytes=VMEM_LIMIT),
        name="ffn",
    )(x, g_ffn, wu, conv_w, wd, g_final)


def _pack_in_proj_weight(w):
    sizes = (ATT_WIDTH, KV_WIDTH, KV_WIDTH, IDX_WIDTH, IDX_DIM, N_IDX_HEADS,
             3 * CONV_CHANNELS, CROSS_WIDTH)
    offs = np.concatenate([[0], np.cumsum(sizes)])
    wq, wk, wv, wqi, wki, wwi, wconv, wqc = [w[:, offs[i]:offs[i + 1]] for i in range(len(sizes))]
    idx_scale = (N_IDX_HEADS * IDX_DIM) ** -0.5
    att_scale = HEAD_DIM ** -0.5
    wwi = jnp.pad(wwi * idx_scale, ((0, 0), (0, LANES - N_IDX_HEADS)))
    w1 = jnp.concatenate([wq * att_scale, wwi, wk, wv, wqi, jnp.tile(wki, (1, N_IDX_HEADS)),
                          wconv, wqc], axis=1)
    return w1.astype(BF16), w[:, offs[-1]:].astype(BF16)


def _layer(x, mem, g_mix, w_in, b_gate, conv_w_short, w_att_out, w_conv_out, w_mem_out, w_o,
           g_mem, w_mem_kv, g_ffn, w_up, conv_w_ffn, w_down, g_final):
    b, s, d = x.shape
    tm, tm_ffn = min(TOKEN_TILE, s), min(FFN_TILE, s)
    assert s % tm == 0 and s % tm_ffn == 0 and tm % Q_TILE == 0 and d % LANES == 0, (s, d)
    w1, wg = _pack_in_proj_weight(w_in)
    q, ks, vt, qi, ki, conv_in, qc, wi = _in_proj(x, g_mix[None], w1, tm, Q_TILE)
    km, vm = _mem_kv(mem, g_mem[None], w_mem_kv.astype(BF16))
    att = _dsa(qi, wi, q, ki, ks, vt)
    x1 = _merge(x, g_mix[None], att, conv_in, qc, km, vm, wg, b_gate[None], conv_w_short,
                w_att_out.astype(BF16), w_conv_out.astype(BF16), w_mem_out.astype(BF16),
                w_o.astype(BF16), tm)
    dff = w_down.shape[0]
    fc = FFN_CHUNK if dff % FFN_CHUNK == 0 else dff
    return _ffn(x1, g_ffn[None], w_up.astype(BF16), conv_w_ffn, w_down.astype(BF16),
                g_final[None], tm_ffn, fc)


def kernel(x, mem, g_mix, w_in, b_gate, conv_w_short, w_att_out, w_conv_out, w_mem_out, w_o,
           g_mem, w_mem_kv, g_ffn, w_up, conv_w_ffn, w_down, g_final):
    depth = w_in.shape[0]
    assert depth == 1, "the final RMSNorm is fused into the last layer's ffn kernel"
    return _layer(x, mem, g_mix[0], w_in[0], b_gate[0], conv_w_short[0], w_att_out[0],
                  w_conv_out[0], w_mem_out[0], w_o[0], g_mem[0], w_mem_kv[0], g_ffn[0],
                  w_up[0], conv_w_ffn[0], w_down[0], g_final)
```

```python
import functools

import numpy as np
import jax
import jax.numpy as jnp
from jax import lax
from jax.experimental import pallas as pl
from jax.experimental.pallas import tpu as pltpu

F32 = jnp.float32
BF16 = jnp.bfloat16
I32 = jnp.int32

HEAD_DIM = 64
N_ATT_HEADS = 6
N_KV_GROUPS = 2
HEADS_PER_GROUP = N_ATT_HEADS // N_KV_GROUPS
ATT_WIDTH = N_ATT_HEADS * HEAD_DIM
KV_WIDTH = N_KV_GROUPS * HEAD_DIM
N_IDX_HEADS = 8
IDX_DIM = 32
IDX_WIDTH = N_IDX_HEADS * IDX_DIM
TOPK_MAX = 256
CONV_CHANNELS = 384
CONV_K = 3
N_CROSS_HEADS = 4
CROSS_WIDTH = N_CROSS_HEADS * HEAD_DIM
N_BRANCHES = 3
ROPE_THETA = 500000.0
ROPE_FRACTION = 4
NORM_EPS = 1e-6

LANES = 128
SUBLANES = 8
VMEM_LIMIT = 56 * 1024 * 1024

TOKEN_TILE = 1024
FFN_TILE = 1024
FFN_CHUNK = 256
Q_TILE = 256
DSA_BATCH = 4
MEM_BATCH = 4
NEG_BIG = -1e30
LOG2_E = 1.4426950408889634
INT_MIN = -2 ** 31
KEY_MIN_FINITE = -2139095040
HI16_MASK = -65536


def _rms(x, g):
    return x * lax.rsqrt(jnp.mean(x * x, axis=-1, keepdims=True) + NORM_EPS) * g


def _dot_nt(a, b):
    return lax.dot_general(a, b, (((1,), (1,)), ((), ())), preferred_element_type=F32)


def _dot(a, b):
    return jnp.dot(a, b, preferred_element_type=F32)


def _rope_tile(x, c, sa, sb, half):
    return x * c + pltpu.roll(x, LANES - half, 1) * sa + pltpu.roll(x, half, 1) * sb


def _rope(x, c, sa, sb, half):
    tiles = [_rope_tile(x[:, i:i + LANES], c, sa, sb, half) for i in range(0, x.shape[1], LANES)]
    return tiles[0] if len(tiles) == 1 else jnp.concatenate(tiles, axis=1)


_OFF_Q = 0
_OFF_WI = _OFF_Q + ATT_WIDTH
_OFF_KV = _OFF_WI + LANES
_OFF_QI = _OFF_KV + 2 * KV_WIDTH
_OFF_KI = _OFF_QI + IDX_WIDTH
_OFF_CONV = _OFF_KI + IDX_WIDTH
_OFF_QC = _OFF_CONV + 3 * CONV_CHANNELS
_W1_WIDTH = _OFF_QC + CROSS_WIDTH


def _in_proj_kernel(x_ref, g_ref, w_ref, ca_ref, saa_ref, sba_ref, ci_ref, sai_ref, sbi_ref,
                    q_ref, ks_ref, vt_ref, qi_ref, ki_ref, conv_ref, qc_ref, wi_ref, *, kchunk):
    h = _rms(x_ref[0], g_ref[...]).astype(BF16)
    ca, saa, sba = ca_ref[...], saa_ref[...], sba_ref[...]
    ci, sai, sbi = ci_ref[...], sai_ref[...], sbi_ref[...]
    half_a = HEAD_DIM // ROPE_FRACTION // 2
    half_i = IDX_DIM // ROPE_FRACTION // 2

    a = _dot(h, w_ref[:, _OFF_Q:_OFF_KV])
    q_ref[0] = (_rope(a[:, :ATT_WIDTH], ca, saa, sba, half_a) * LOG2_E).astype(BF16)
    wi_ref[0] = a[:, ATT_WIDTH:]

    kv = _dot(h, w_ref[:, _OFF_KV:_OFF_QI])
    ks_ref[0] = _rope(kv[:, :KV_WIDTH], ca, saa, sba, half_a).astype(BF16)
    v = kv[:, KV_WIDTH:]
    for i in range(v.shape[0] // kchunk):
        vt_ref[0, i] = v[i * kchunk:(i + 1) * kchunk].T.astype(BF16)

    idx = _dot(h, w_ref[:, _OFF_QI:_OFF_CONV])
    qi_ref[0] = _rope(idx[:, :IDX_WIDTH], ci, sai, sbi, half_i).astype(BF16)
    ki_ref[0] = _rope(idx[:, IDX_WIDTH:], ci, sai, sbi, half_i).astype(BF16)

    cq = _dot(h, w_ref[:, _OFF_CONV:_W1_WIDTH])
    conv_ref[0] = cq[:, :3 * CONV_CHANNELS].astype(BF16)
    qc_ref[0] = cq[:, 3 * CONV_CHANNELS:].astype(BF16)


def _rope_tables(s, period, rot_dim):
    half = rot_dim // 2
    inv_freq = ROPE_THETA ** (-jnp.arange(half, dtype=F32) / half)
    ang = jnp.arange(s).astype(F32)[:, None] * inv_freq[None, :]
    cos, sin = jnp.cos(ang), jnp.sin(ang)
    j = np.arange(LANES) % period
    first = jnp.asarray(j < half)[None, :]
    second = jnp.asarray((j >= half) & (j < rot_dim))[None, :]
    fidx = np.where(j < half, j, np.where(j < rot_dim, j - half, 0))
    cos_l, sin_l = cos[:, fidx], sin[:, fidx]
    c = jnp.where(first | second, cos_l, 1.0)
    sa = jnp.where(first, -sin_l, 0.0)
    sb = jnp.where(second, sin_l, 0.0)
    return c, sa, sb


def _in_proj(x, g_mix, w1, tm, kchunk):
    b, s, d = x.shape
    tabs = _rope_tables(s, HEAD_DIM, HEAD_DIM // ROPE_FRACTION) + \
        _rope_tables(s, IDX_DIM, IDX_DIM // ROPE_FRACTION)
    tok = lambda w: pl.BlockSpec((1, tm, w), lambda si, bi: (bi, si, 0))
    tab = pl.BlockSpec((tm, LANES), lambda si, bi: (si, 0))
    const = lambda shape: pl.BlockSpec(shape, lambda si, bi: (0,) * len(shape),
                                       pipeline_mode=pl.Buffered(1))
    out_shape = (
        jax.ShapeDtypeStruct((b, s, ATT_WIDTH), BF16),
        jax.ShapeDtypeStruct((b, s, KV_WIDTH), BF16),
        jax.ShapeDtypeStruct((b, s // kchunk, KV_WIDTH, kchunk), BF16),
        jax.ShapeDtypeStruct((b, s, IDX_WIDTH), BF16),
        jax.ShapeDtypeStruct((b, s, IDX_WIDTH), BF16),
        jax.ShapeDtypeStruct((b, s, 3 * CONV_CHANNELS), BF16),
        jax.ShapeDtypeStruct((b, s, CROSS_WIDTH), BF16),
        jax.ShapeDtypeStruct((b, s, LANES), F32),
    )
    out_specs = (tok(ATT_WIDTH), tok(KV_WIDTH),
                 pl.BlockSpec((1, tm // kchunk, KV_WIDTH, kchunk), lambda si, bi: (bi, si, 0, 0)),
                 tok(IDX_WIDTH), tok(IDX_WIDTH), tok(3 * CONV_CHANNELS), tok(CROSS_WIDTH), tok(LANES))
    return pl.pallas_call(
        functools.partial(_in_proj_kernel, kchunk=kchunk),
        grid=(s // tm, b),
        in_specs=[tok(d), const((1, d)), const((d, _W1_WIDTH))] + [tab] * 6,
        out_specs=out_specs,
        out_shape=out_shape,
        compiler_params=pltpu.CompilerParams(
            dimension_semantics=("arbitrary", "arbitrary"), vmem_limit_bytes=VMEM_LIMIT),
        name="in_proj",
    )(x, g_mix, w1, *tabs)


def _mem_kv_kernel(mem_ref, g_ref, w_ref, km_ref, vm_ref):
    nb, m, d = mem_ref.shape
    h = _rms(mem_ref[...].reshape(nb * m, d), g_ref[...]).astype(BF16)
    kv = _dot(h, w_ref[...])
    km_ref[...] = kv[:, :CROSS_WIDTH].astype(BF16).reshape(nb, m, CROSS_WIDTH)
    vm_ref[...] = kv[:, CROSS_WIDTH:].astype(BF16).reshape(nb, m, CROSS_WIDTH)


def _mem_kv(mem, g_mem, w_kv):
    b, m, d = mem.shape
    nb = MEM_BATCH if b % MEM_BATCH == 0 else 1
    const = lambda shape: pl.BlockSpec(shape, lambda bi: (0,) * len(shape))
    spec = pl.BlockSpec((nb, m, CROSS_WIDTH), lambda bi: (bi, 0, 0))
    return pl.pallas_call(
        _mem_kv_kernel,
        grid=(b // nb,),
        in_specs=[pl.BlockSpec((nb, m, d), lambda bi: (bi, 0, 0)), const((1, d)),
                  const((d, 2 * CROSS_WIDTH))],
        out_specs=(spec, spec),
        out_shape=(jax.ShapeDtypeStruct((b, m, CROSS_WIDTH), BF16),) * 2,
        compiler_params=pltpu.CompilerParams(dimension_semantics=("arbitrary",)),
        name="mem_kv",
    )(mem, g_mem, w_kv)


def _key_to_float(key):
    return lax.bitcast_convert_type(key ^ ((key >> 31) & 0x7FFFFFFF), F32)


def _col_reduce(x, op):
    rows, n = x.shape
    r = op(x.reshape(SUBLANES, rows // (SUBLANES * SUBLANES), SUBLANES, n), axis=1)
    return op(op(r, axis=0), axis=0, keepdims=True)


def _for_chunks(n, body, init, slab=False):
    def pair(i, carry):
        if slab:
            return body(2 * i, carry, 2)
        return body(2 * i + 1, body(2 * i, carry))
    carry = lax.fori_loop(0, n // 2, pair, init)
    return lax.cond(n % 2 == 1, lambda cr: body(n - 1, cr), lambda cr: cr, carry)


def _dsa_kernel(qi_ref, wi_ref, q_ref, ki_ref, ks_ref, vt_ref, o_ref, sc_ref, sh_ref, acc_ref,
                *, topk, jbits):
    tq = ch = Q_TILE
    batch = range(qi_ref.shape[0])
    qb = pl.program_id(1)
    nc = qb + 1
    qpos = qb * tq + lax.broadcasted_iota(I32, (1, tq), 1)

    def chunk_pos(c):
        off = pl.multiple_of(c * ch, ch)
        return off, off + lax.broadcasted_iota(I32, (ch, tq), 0)

    head_of_lane = lax.broadcasted_iota(I32, (tq, IDX_WIDTH), 1) // IDX_DIM
    zero_bf = jnp.zeros((), BF16)
    qi_heads = [[jnp.where(head_of_lane == h, qi_ref[bb], zero_bf) for h in range(N_IDX_HEADS)]
                for bb in batch]
    wit = [wi_ref[bb].T for bb in batch]
    wrow = [[wit[bb][h:h + 1, :] for h in range(N_IDX_HEADS)] for bb in batch]

    def scores(bb, c, nchunks=1):
        off = pl.multiple_of(c * ch, ch)
        kc = ki_ref[bb, pl.ds(off, nchunks * ch), :]
        acc = jnp.maximum(_dot_nt(kc, qi_heads[bb][0]), 0.0) * wrow[bb][0]
        for h in range(1, N_IDX_HEADS):
            acc = acc + jnp.maximum(_dot_nt(kc, qi_heads[bb][h]), 0.0) * wrow[bb][h]
        return off, acc

    def put_scores(bb, off, s):
        sc_ref[bb, pl.ds(off, s.shape[0]), :] = s
        hi = lax.bitcast_convert_type(lax.bitcast_convert_type(s, I32) & HI16_MASK, F32)
        sh_ref[bb, pl.ds(off, s.shape[0]), :] = hi.astype(BF16)

    def score_chunks(c, carry, nchunks=1):
        for bb in batch:
            put_scores(bb, *scores(bb, c, nchunks))
        return carry

    _for_chunks(qb, score_chunks, 0, slab=True)
    for bb in batch:
        off, acc = scores(bb, qb)
        put_scores(bb, off, jnp.where(chunk_pos(qb)[1] <= qpos, acc, -jnp.inf))

    def count_ge(ref, cands):
        rows = SUBLANES * (4 // ref.dtype.itemsize)
        one, zero = jnp.ones((), ref.dtype), jnp.zeros((), ref.dtype)

        def body(c, cnts):
            off, _ = chunk_pos(c)
            out = []
            for bb in batch:
                hit = jnp.where(ref[bb, pl.ds(off, ch), :] >= cands[bb], one, zero)
                parts = [hit[r:r + rows] for r in range(0, ch, rows)]
                while len(parts) > 1:
                    parts = [a + b for a, b in zip(parts[::2], parts[1::2])]
                out.append(cnts[bb] + parts[0].astype(F32))
            return tuple(out)

        cnts = _for_chunks(nc, body, tuple(jnp.zeros((rows, tq), F32) for _ in batch))
        return [cnt.sum(axis=0, keepdims=True) for cnt in cnts]

    def key_bit(i, carry, ref, to_cand):
        keys, n_keys = carry
        bit = lax.shift_left(jnp.int32(1), 31 - i)
        trials = [key + bit for key in keys]
        n_trials = count_ge(ref, [to_cand(trial) for trial in trials])
        keeps = [n_trial >= topk for n_trial in n_trials]
        return (tuple(jnp.where(keeps[bb], trials[bb], keys[bb]) for bb in batch),
                tuple(jnp.where(keeps[bb], n_trials[bb], n_keys[bb]) for bb in batch))

    def hi_cand(trial):
        pattern = trial ^ ((trial >> 31) & 0x7FFF0000)
        return lax.bitcast_convert_type(pattern, F32).astype(BF16)

    def count(pred):
        def body(c, cnts):
            off, kpos = chunk_pos(c)
            hits = [jnp.where(pred(bb, sc_ref[bb, pl.ds(off, ch), :], kpos), 1.0, 0.0) for bb in batch]
            return tuple(cnts[bb] + hits[bb].reshape(ch // SUBLANES, SUBLANES, tq).sum(axis=0)
                         for bb in batch)
        cnts = _for_chunks(nc, body, tuple(jnp.zeros((SUBLANES, tq), F32) for _ in batch))
        return [cnt.sum(axis=0, keepdims=True) for cnt in cnts]

    def search(carry):
        carry = lax.fori_loop(0, 16, lambda i, cr: key_bit(i, cr, sh_ref, hi_cand), carry)
        return lax.fori_loop(16, 32, lambda i, cr: key_bit(i, cr, sc_ref, _key_to_float), carry)

    init = (tuple(jnp.full((1, tq), INT_MIN, I32) for _ in batch),
            tuple(jnp.zeros((1, tq), F32) for _ in batch))
    keys, n_keys = lax.cond(nc * tq <= topk, lambda cr: cr, search, init)
    taus = [_key_to_float(jnp.maximum(key, KEY_MIN_FINITE)) for key in keys]

    excess = sum(jnp.where(n_key > topk, 1, 0) for n_key in n_keys)

    @pl.when(jnp.max(excess) > 0)
    def _():
        n_above = count(lambda bb, s, _: s > taus[bb])
        wants = [topk - n for n in n_above]

        def pos_bit(i, jmaxs):
            bit = lax.shift_left(jnp.int32(1), jbits - 1 - i)
            trials = [jmax + bit for jmax in jmaxs]
            n_before = count(lambda bb, s, kpos: jnp.where(s == taus[bb], kpos, trials[bb]) < trials[bb])
            return tuple(jnp.where(n_before[bb] <= wants[bb], trials[bb], jmaxs[bb]) for bb in batch)

        jmaxs = lax.fori_loop(0, jbits, pos_bit, tuple(jnp.zeros((1, tq), I32) for _ in batch))

        def drop(c, carry):
            off, kpos = chunk_pos(c)
            for bb in batch:
                s = sc_ref[bb, pl.ds(off, ch), :]
                late_tie = jnp.where(s == taus[bb], kpos, -1) >= jmaxs[bb]
                sc_ref[bb, pl.ds(off, ch), :] = jnp.where(late_tie, -jnp.inf, s)
            return carry

        lax.fori_loop(0, nc, drop, 0)

    group_of_lane = lax.broadcasted_iota(I32, (tq, LANES), 1) // HEAD_DIM
    qe = []
    for bb in batch:
        q = q_ref[bb]
        qe.append([])
        for h in range(N_ATT_HEADS):
            g = h // HEADS_PER_GROUP
            t = q[:, (h // 2) * LANES:(h // 2 + 1) * LANES].astype(F32)
            if h % 2 != g:
                t = pltpu.roll(t, HEAD_DIM, 1)
            qe[bb].append(jnp.where(group_of_lane == g, t, 0.0).astype(BF16))

    acc_ref[...] = jnp.zeros(acc_ref.shape, F32)
    streams = [(bb, h) for bb in batch for h in range(N_ATT_HEADS)]

    def attend(c, carry, nchunks=1):
        ms, ls = carry
        off = pl.multiple_of(c * ch, ch)
        rows = nchunks * ch
        kcs = [ks_ref[bb, pl.ds(off, rows), :] for bb in batch]
        lgs = [_dot_nt(kcs[bb], qe[bb][h]) for bb, h in streams]
        vts, biases = [], []
        for bb in batch:
            vt = vt_ref[bb, c]
            if nchunks == 2:
                vt = jnp.concatenate([vt, vt_ref[bb, c + 1]], axis=1)
            vts.append(vt)
            biases.append(jnp.where(sc_ref[bb, pl.ds(off, rows), :] >= taus[bb], 0.0, NEG_BIG))
        new_ms, new_ls = [], []
        for i, (bb, h) in enumerate(streams):
            lg = biases[bb] + lgs[i]
            m_new = jnp.maximum(ms[i], _col_reduce(lg, jnp.max))
            alpha = jnp.exp2(ms[i] - m_new)
            p = jnp.exp2(lg - m_new)
            new_ms.append(m_new)
            new_ls.append(alpha * ls[i] + _col_reduce(p, jnp.sum))
            cols = slice(h * tq, (h + 1) * tq)
            acc_ref[bb, :, cols] = alpha * acc_ref[bb, :, cols] + _dot(vts[bb], p.astype(BF16))
        return tuple(new_ms), tuple(new_ls)

    init = (tuple(jnp.full((1, tq), NEG_BIG, F32) for _ in streams),
            tuple(jnp.zeros((1, tq), F32) for _ in streams))
    _, ls = _for_chunks(nc, attend, init, slab=True)

    lane = lax.broadcasted_iota(I32, (tq, LANES), 1)
    for bb in batch:
        tiles = []
        for j in range(N_ATT_HEADS // 2):
            parts = []
            for p_ in range(2):
                h = 2 * j + p_
                blk = acc_ref[bb, :, h * tq:(h + 1) * tq] * (1.0 / ls[streams.index((bb, h))])
                if h // HEADS_PER_GROUP != p_:
                    blk = jnp.concatenate([blk[HEAD_DIM:], blk[:HEAD_DIM]], axis=0)
                parts.append(blk.T)
            tiles.append(jnp.where(lane < HEAD_DIM, parts[0], parts[1]))
        o_ref[bb] = jnp.concatenate(tiles, axis=1).astype(BF16)


def _dsa(qi, wi, q, ki, ks, vt):
    b, s, _ = q.shape
    topk = min(TOPK_MAX, s // 4)
    nb = DSA_BATCH if b % DSA_BATCH == 0 else 1
    blk = lambda w: pl.BlockSpec((nb, Q_TILE, w), lambda bi, qb: (bi, qb, 0))
    full = lambda w: pl.BlockSpec((nb, s, w), lambda bi, qb: (bi, 0, 0))
    return pl.pallas_call(
        functools.partial(_dsa_kernel, topk=topk, jbits=int(s).bit_length()),
        grid=(b // nb, s // Q_TILE),
        in_specs=[blk(IDX_WIDTH), blk(LANES), blk(ATT_WIDTH), full(IDX_WIDTH), full(KV_WIDTH),
                  pl.BlockSpec((nb, s // Q_TILE, KV_WIDTH, Q_TILE), lambda bi, qb: (bi, 0, 0, 0))],
        out_specs=blk(ATT_WIDTH),
        out_shape=jax.ShapeDtypeStruct((b, s, ATT_WIDTH), BF16),
        scratch_shapes=[pltpu.VMEM((nb, s, Q_TILE), F32), pltpu.VMEM((nb, s, Q_TILE), BF16),
                        pltpu.VMEM((nb, KV_WIDTH, N_ATT_HEADS * Q_TILE), F32)],
        compiler_params=pltpu.CompilerParams(
            dimension_semantics=("arbitrary", "arbitrary"), vmem_limit_bytes=VMEM_LIMIT),
        name="dsa",
    )(qi, wi, q, ki, ks, vt)


def _causal_conv3(u, w, carry_ref):
    prev = carry_ref[...]
    row = lax.broadcasted_iota(I32, prev.shape, 0)
    s1 = pltpu.roll(u, 1, 0)
    s2 = pltpu.roll(u, 2, 0)
    p1 = pltpu.roll(prev, 1, 0)
    p2 = pltpu.roll(prev, 2, 0)
    s1 = jnp.concatenate([jnp.where(row < 1, p1, s1[:SUBLANES]), s1[SUBLANES:]], axis=0)
    s2 = jnp.concatenate([jnp.where(row < 2, p2, s2[:SUBLANES]), s2[SUBLANES:]], axis=0)
    carry_ref[...] = u[u.shape[0] - SUBLANES:]
    return s2 * w[0:1] + s1 * w[1:2] + u * w[2:3]


def _merge_kernel(x_ref, g_ref, att_ref, conv_ref, qc_ref, km_ref, vm_ref, wg_ref, bg_ref, cw_ref,
                  wa_ref, wc_ref, wm_ref, wo_ref, o_ref, carry_ref):
    @pl.when(pl.program_id(1) == 0)
    def _():
        carry_ref[...] = jnp.zeros(carry_ref.shape, F32)

    x = x_ref[0]
    h = _rms(x, g_ref[...]).astype(BF16)

    y_att = _dot(att_ref[0], wa_ref[...])

    cin = conv_ref[0].astype(F32)
    c = CONV_CHANNELS
    cu = cin[:, c:2 * c] * cin[:, 2 * c:]
    yc = cin[:, :c] * _causal_conv3(cu, cw_ref[...], carry_ref)
    y_conv = _dot(yc.astype(BF16), wc_ref[...])

    qc, km, vm = qc_ref[0], km_ref[0], vm_ref[0]
    head_of_lane = lax.broadcasted_iota(I32, km.shape, 1) // HEAD_DIM
    zero_bf = jnp.zeros((), BF16)
    mem_out = jnp.zeros((x.shape[0], CROSS_WIDTH), F32)
    for hh in range(N_CROSS_HEADS):
        lg = _dot_nt(qc, jnp.where(head_of_lane == hh, km, zero_bf)) * (HEAD_DIM ** -0.5)
        e = jnp.exp(lg - lg.max(axis=-1, keepdims=True))
        p = e / e.sum(axis=-1, keepdims=True)
        mem_out = mem_out + _dot(p.astype(BF16), jnp.where(head_of_lane == hh, vm, zero_bf))
    y_mem = _dot(mem_out.astype(BF16), wm_ref[...])

    d = x.shape[1]
    merged = jnp.zeros(x.shape, F32)
    for i, y in enumerate((y_att, y_conv, y_mem)):
        pre = bg_ref[:, i * d:(i + 1) * d] + _dot(h, wg_ref[:, i * d:(i + 1) * d])
        merged = merged + jax.nn.sigmoid(pre) * y
    o_ref[0] = x + _dot(merged.astype(BF16), wo_ref[...])


def _merge(x, g_mix, att, conv_in, qc, km, vm, wg, b_gate, conv_w, wa, wc, wm, wo, tm):
    b, s, d = x.shape
    m = km.shape[1]
    tok = lambda w: pl.BlockSpec((1, tm, w), lambda bi, si: (bi, si, 0))
    const = lambda shape: pl.BlockSpec(shape, lambda bi, si: (0,) * len(shape),
                                       pipeline_mode=pl.Buffered(1))
    memspec = pl.BlockSpec((1, m, CROSS_WIDTH), lambda bi, si: (bi, 0, 0))
    return pl.pallas_call(
        _merge_kernel,
        grid=(b, s // tm),
        in_specs=[tok(d), const((1, d)), tok(ATT_WIDTH), tok(3 * CONV_CHANNELS), tok(CROSS_WIDTH),
                  memspec, memspec, const((d, N_BRANCHES * d)), const((1, N_BRANCHES * d)),
                  const((CONV_K, CONV_CHANNELS)), const((ATT_WIDTH, d)), const((CONV_CHANNELS, d)),
                  const((CROSS_WIDTH, d)), const((d, d))],
        out_specs=tok(d),
        out_shape=jax.ShapeDtypeStruct((b, s, d), F32),
        scratch_shapes=[pltpu.VMEM((SUBLANES, CONV_CHANNELS), F32)],
        compiler_params=pltpu.CompilerParams(
            dimension_semantics=("arbitrary", "arbitrary"), vmem_limit_bytes=VMEM_LIMIT),
        name="merge",
    )(x, g_mix, att, conv_in, qc, km, vm, wg, b_gate, conv_w, wa, wc, wm, wo)


def _ffn_kernel(x_ref, g_ref, wu_ref, cw_ref, wd_ref, gf_ref, o_ref, carry_ref, act_ref, *, fc):
    @pl.when(pl.program_id(1) == 0)
    def _():
        carry_ref[...] = jnp.zeros(carry_ref.shape, F32)

    x = x_ref[0]
    h = _rms(x, g_ref[...]).astype(BF16)
    dff = wd_ref.shape[0]
    for c0 in range(0, dff, fc):
        halves = []
        for base in (0, dff):
            cols = slice(base + c0, base + c0 + fc)
            up = _dot(h, wu_ref[:, cols])
            halves.append(_causal_conv3(up, cw_ref[:, cols], carry_ref.at[:, cols]))
        gate, val = halves
        act_ref[:, c0:c0 + fc] = (gate * jax.nn.sigmoid(gate) * val).astype(BF16)
    y = x + _dot(act_ref[...], wd_ref[...])
    o_ref[0] = _rms(y, gf_ref[...])


def _ffn(x, g_ffn, wu, conv_w, wd, g_final, tm, fc):
    b, s, d = x.shape
    dff = wd.shape[0]
    tok = pl.BlockSpec((1, tm, d), lambda bi, si: (bi, si, 0))
    const = lambda shape: pl.BlockSpec(shape, lambda bi, si: (0,) * len(shape),
                                       pipeline_mode=pl.Buffered(1))
    return pl.pallas_call(
        functools.partial(_ffn_kernel, fc=fc),
        grid=(b, s // tm),
        in_specs=[tok, const((1, d)), const((d, 2 * dff)), const((CONV_K, 2 * dff)),
                  const((dff, d)), const((1, d))],
        out_specs=tok,
        out_shape=jax.ShapeDtypeStruct((b, s, d), F32),
        scratch_shapes=[pltpu.VMEM((SUBLANES, 2 * dff), F32), pltpu.VMEM((tm, dff), BF16)],
        compiler_params=pltpu.CompilerParams(
            dimension_semantics=("arbitrary", "arbitrary"), vmem_limit_bytes=VMEM_LIMIT),
        name="ffn",
    )(x, g_ffn, wu, conv_w, wd, g_final)


def _pack_in_proj_weight(w):
    sizes = (ATT_WIDTH, KV_WIDTH, KV_WIDTH, IDX_WIDTH, IDX_DIM, N_IDX_HEADS,
             3 * CONV_CHANNELS, CROSS_WIDTH)
    offs = np.concatenate([[0], np.cumsum(sizes)])
    w = w.astype(BF16)
    wq, wk, wv, wqi, wki, wwi, wconv, wqc = [w[:, offs[i]:offs[i + 1]] for i in range(len(sizes))]
    idx_scale = (N_IDX_HEADS * IDX_DIM) ** -0.5
    att_scale = HEAD_DIM ** -0.5
    wwi = jnp.pad(wwi * idx_scale, ((0, 0), (0, LANES - N_IDX_HEADS)))
    w1 = jnp.concatenate([wq * att_scale, wwi, wk, wv, wqi, jnp.tile(wki, (1, N_IDX_HEADS)),
                          wconv, wqc], axis=1)
    return w1, w[:, offs[-1]:]


def _layer(x, mem, g_mix, w_in, b_gate, conv_w_short, w_att_out, w_conv_out, w_mem_out, w_o,
           g_mem, w_mem_kv, g_ffn, w_up, conv_w_ffn, w_down, g_final):
    b, s, d = x.shape
    tm, tm_ffn = min(TOKEN_TILE, s), min(FFN_TILE, s)
    assert s % tm == 0 and s % tm_ffn == 0 and tm % Q_TILE == 0 and d % LANES == 0, (s, d)
    w1, wg = _pack_in_proj_weight(w_in)
    q, ks, vt, qi, ki, conv_in, qc, wi = _in_proj(x, g_mix[None], w1, tm, Q_TILE)
    km, vm = _mem_kv(mem, g_mem[None], w_mem_kv.astype(BF16))
    att = _dsa(qi, wi, q, ki, ks, vt)
    x1 = _merge(x, g_mix[None], att, conv_in, qc, km, vm, wg, b_gate[None], conv_w_short,
                w_att_out.astype(BF16), w_conv_out.astype(BF16), w_mem_out.astype(BF16),
                w_o.astype(BF16), tm)
    dff = w_down.shape[0]
    fc = FFN_CHUNK if dff % FFN_CHUNK == 0 else dff
    return _ffn(x1, g_ffn[None], w_up.astype(BF16), conv_w_ffn, w_down.astype(BF16),
                g_final[None], tm_ffn, fc)


def kernel(x, mem, g_mix, w_in, b_gate, conv_w_short, w_att_out, w_conv_out, w_mem_out, w_o,
           g_mem, w_mem_kv, g_ffn, w_up, conv_w_ffn, w_down, g_final):
    depth = w_in.shape[0]
    assert depth == 1, "the final RMSNorm is fused into the last layer's ffn kernel"
    return _layer(x, mem, g_mix[0], w_in[0], b_gate[0], conv_w_short[0], w_att_out[0],
                  w_conv_out[0], w_mem_out[0], w_o[0], g_mem[0], w_mem_kv[0], g_ffn[0],
                  w_up[0], conv_w_ffn[0], w_down[0], g_final)
```
